```python
import math
import jax
import jax.numpy as jnp
from jax import lax
import numpy as np

D_MODEL = 2048
BATCH = 1
SEQ = 8192
DEPTH = 1

GRID_W = 64
CTX_LEN = 256
EPS = 1e-6

ML_HEADS = 8
ML_DQK = D_MODEL // 16
ML_DV = D_MODEL // 8
ML_QK_W = ML_HEADS * ML_DQK
ML_V_W = ML_HEADS * ML_DV
ML_CHUNK = 64

HY_W = D_MODEL
HY_ORDER = 2
HY_SHORT = 3
HY_BANDS = 8
HY_FEAT = 1 + 2 * HY_BANDS
HY_FFN = 64
HY_SHIFT = 0.05
HY_MIN_DECAY = math.log(1e-2) / 1.5
HY_MAX_DECAY = math.log(1e-2) / 0.3

PEER_HEADS = 8
PEER_NKEYS = 128
PEER_N = PEER_NKEYS * PEER_NKEYS
PEER_DKEY = 256
PEER_TOPK = 16
PEER_BLOCK = 128

IN_W = 2 * ML_QK_W + 2 * ML_V_W + 4 * ML_HEADS + 3 * HY_W + 2 * D_MODEL
IN_SPLIT = (2 * ML_QK_W,
            2 * ML_QK_W + ML_V_W,
            2 * ML_QK_W + 2 * ML_V_W,
            2 * ML_QK_W + 2 * ML_V_W + 4 * ML_HEADS,
            2 * ML_QK_W + 2 * ML_V_W + 4 * ML_HEADS + 3 * HY_W)

kernel_name = 'hybrid_mlstm_hyena_peer_flow_block'


def rmsnorm(x, g):
    xf = x.astype(jnp.float32)
    y = xf * lax.rsqrt(jnp.mean(xf * xf, axis=-1, keepdims=True) + EPS)
    return y.astype(x.dtype) * g


def dwconv1d(x, w):
    K = w.shape[0]
    L = x.shape[1]
    pad = K // 2
    xp = jnp.pad(x, ((0, 0), (pad, pad), (0, 0)))
    y = xp[:, 0:L] * w[0]
    for j in range(1, K):
        y = y + xp[:, j:j + L] * w[j]
    return y


def dwconv_grid(x, w, rows):
    B, L, C = x.shape
    img = x.reshape(B, rows, GRID_W, C)
    y = lax.conv_general_dilated(img, w[:, :, None, :], (1, 1), 'SAME',
                                 dimension_numbers=('NHWC', 'HWIO', 'NHWC'),
                                 feature_group_count=C)
    return y.reshape(B, L, C)


def _flip(a):
    return jnp.flip(a, axis=2)


def mlstm_heads(qk, v, gates, gate_b):
    B, L, _ = v.shape
    q, k = jnp.split(qk, 2, axis=-1)
    q = q.reshape(B, L, ML_HEADS, ML_DQK).transpose(0, 2, 1, 3).astype(jnp.float32) * (ML_DQK ** -0.5)
    k = k.reshape(B, L, ML_HEADS, ML_DQK).transpose(0, 2, 1, 3).astype(jnp.float32)
    vh = v.reshape(B, L, ML_HEADS, ML_DV).transpose(0, 2, 1, 3).astype(jnp.float32)
    g = gates.astype(jnp.float32).reshape(B, L, 4, ML_HEADS).transpose(2, 0, 3, 1) \
        + gate_b.astype(jnp.float32)[:, None, :, None]
    g_fwd = (g[0], jax.nn.log_sigmoid(g[1]))
    g_bwd = (g[2], jax.nn.log_sigmoid(g[3]))
    return q, k, vh, g_fwd, g_bwd


def mlstm_zero_state(B):
    return (jnp.zeros((B, ML_HEADS, ML_DQK, ML_DV), jnp.float32),
            jnp.zeros((B, ML_HEADS, ML_DQK), jnp.float32),
            jnp.zeros((B, ML_HEADS), jnp.float32))


def mlstm_final_state(k, v, log_i, log_f):
    b = jnp.cumsum(log_f, axis=-1)
    dec = b[..., -1:] - b + log_i
    m = jnp.max(dec, axis=-1)
    w = jnp.exp(dec - m[..., None])
    C = jnp.einsum('bhs,bhsd,bhsv->bhdv', w, k, v)
    n = jnp.einsum('bhs,bhsd->bhd', w, k)
    return (C, n, m)


def mlstm_chunkwise(q, k, v, log_i, log_f, state):
    B, H, L, _ = q.shape
    T = ML_CHUNK
    nc = L // T

    def to_chunks(a):
        return jnp.moveaxis(a.reshape(a.shape[:2] + (nc, T) + a.shape[3:]), 2, 0)

    causal = jnp.tril(jnp.ones((T, T), dtype=bool))

    def step(carry, inp):
        C, n, m = carry
        qb, kb, vb, ib, fb = inp
        b = jnp.cumsum(fb, axis=-1)
        D = b[..., :, None] - b[..., None, :] + ib[..., None, :]
        D = jnp.where(causal, D, -jnp.inf)
        inter = b + m[..., None]
        m_t = jnp.maximum(inter, jnp.max(D, axis=-1))
        w = jnp.exp(D - m_t[..., None])
        s_inter = jnp.exp(inter - m_t)
        qk = jnp.einsum('bhtd,bhsd->bhts', qb, kb) * w
        num = jnp.einsum('bhts,bhsv->bhtv', qk, vb) \
            + s_inter[..., None] * jnp.einsum('bhtd,bhdv->bhtv', qb, C)
        den = jnp.sum(qk, axis=-1) + s_inter * jnp.einsum('bhtd,bhd->bht', qb, n)
        h = num / jnp.maximum(jnp.abs(den), jnp.exp(-m_t))[..., None]
        bT = b[..., -1]
        dec = bT[..., None] - b + ib
        m_new = jnp.maximum(bT + m, jnp.max(dec, axis=-1))
        wk = jnp.exp(dec - m_new[..., None])
        s_old = jnp.exp(bT + m - m_new)
        C_new = s_old[..., None, None] * C + jnp.einsum('bhs,bhsd,bhsv->bhdv', wk, kb, vb)
        n_new = s_old[..., None] * n + jnp.einsum('bhs,bhsd->bhd', wk, kb)
        return (C_new, n_new, m_new), h

    _, h = lax.scan(step, state, (to_chunks(q), to_chunks(k), to_chunks(v),
                                  to_chunks(log_i), to_chunks(log_f)))
    return jnp.moveaxis(h, 0, 2).reshape(B, H, L, v.shape[-1])


def mlstm_bidir(q, k, v, g_fwd, g_bwd, st_f, st_b):
    h_f = mlstm_chunkwise(q, k, v, g_fwd[0], g_fwd[1], st_f)
    h_b = mlstm_chunkwise(_flip(q), _flip(k), _flip(v), _flip(g_bwd[0]), _flip(g_bwd[1]), st_b)
    return h_f + _flip(h_b)


def mlstm_out(h, o, g):
    B, H, L, dv = h.shape
    h = h * lax.rsqrt(jnp.mean(h * h, axis=-1, keepdims=True) + EPS)
    h = h.transpose(0, 2, 1, 3).reshape(B, L, H * dv).astype(o.dtype)
    return h * g * jax.nn.sigmoid(o)


def hyena_filters(L, w1, b1, w2, b2, w3, freq):
    t = jnp.arange(L, dtype=jnp.float32)
    tn = t / L
    bands = jnp.linspace(1e-4, HY_BANDS - 1, HY_BANDS, dtype=jnp.float32)
    ang = (2.0 * math.pi / L) * t[:, None] * bands[None, :]
    feats = jnp.concatenate([tn[:, None], jnp.cos(ang), -jnp.sin(ang)], axis=-1)
    hdn = jnp.sin(freq * (feats @ w1 + b1))
    hdn = jnp.sin(freq * (hdn @ w2 + b2))
    filt = (hdn @ w3).astype(jnp.float32).reshape(L, 2, HY_ORDER, HY_W)
    deltas = jnp.abs(jnp.linspace(HY_MIN_DECAY, HY_MAX_DECAY, HY_W, dtype=jnp.float32))
    window = jnp.exp(-tn[:, None] * deltas[None, :]) + HY_SHIFT
    return filt * window[:, None, None, :]


def long_conv(z, h_fwd, h_bwd, bias):
    L = z.shape[1]
    n = 2 * L
    Z = jnp.fft.rfft(z.astype(jnp.float32), n=n, axis=1)
    Hf = jnp.fft.rfft(h_fwd, n=n, axis=0)
    Hb = jnp.fft.rfft(h_bwd, n=n, axis=0)
    y = jnp.fft.irfft(Z * (Hf + jnp.conj(Hb))[None], n=n, axis=1)[:, :L]
    return (y + z.astype(jnp.float32) * bias.astype(jnp.float32)).astype(z.dtype)


def hyena(xs, filt, bias):
    x1, x2, z = jnp.split(xs, 3, axis=-1)
    z = x1 * long_conv(z, filt[:, 0, 0], filt[:, 1, 0], bias[0])
    z = x2 * long_conv(z, filt[:, 0, 1], filt[:, 1, 1], bias[1])
    return z


def merge(h_ml, h_hy, bg, w_pm, w_ph, w_o):
    g_ml, g_hy = jnp.split(bg, 2, axis=-1)
    y = jax.nn.sigmoid(g_ml) * (h_ml @ w_pm) + jax.nn.sigmoid(g_hy) * (h_hy @ w_ph)
    return y @ w_o


def peer(u, w_q, sub_keys, expert_u, expert_v):
    B, L, D = u.shape
    blocks = u.reshape(B * L // PEER_BLOCK, PEER_BLOCK, D)

    def one_block(ub):
        q = (ub @ w_q).reshape(PEER_BLOCK, PEER_HEADS, 2, PEER_DKEY // 2)
        s = jnp.einsum('thpd,hpkd->thpk', q, sub_keys).astype(jnp.float32)
        s1, i1 = lax.top_k(s[:, :, 0], PEER_TOPK)
        s2, i2 = lax.top_k(s[:, :, 1], PEER_TOPK)
        cand = (s1[..., :, None] + s2[..., None, :]).reshape(PEER_BLOCK, PEER_HEADS, PEER_TOPK * PEER_TOPK)
        cand_idx = (i1[..., :, None] * PEER_NKEYS + i2[..., None, :]).reshape(PEER_BLOCK, PEER_HEADS, PEER_TOPK * PEER_TOPK)
        best, pos = lax.top_k(cand, PEER_TOPK)
        e = jnp.take_along_axis(cand_idx, pos, axis=-1)
        g = jax.nn.softmax(best, axis=-1).astype(ub.dtype)
        act = jax.nn.gelu(jnp.einsum('td,thkd->thk', ub, expert_u[e]), approximate=False) * g
        return jnp.einsum('thk,thkd->td', act, expert_v[e])

    return lax.map(one_block, blocks).reshape(B, L, D)


def setup_inputs(seed: int = 0) -> dict:
    key = jax.random.key(seed)
    ks = jax.random.split(key, 32)

    def nrm(k, shape, s):
        return jax.random.normal(k, shape, jnp.float32) * s

    ib = nrm(ks[10], (DEPTH, 2, ML_HEADS), 0.1)
    fb = 3.0 + nrm(ks[11], (DEPTH, 2, ML_HEADS), 0.5)
    ml_gate_b = jnp.stack([ib[:, 0], fb[:, 0], ib[:, 1], fb[:, 1]], axis=1)
    return {
        'x': nrm(ks[0], (BATCH, SEQ, D_MODEL), 1.0),
        'c': nrm(ks[1], (BATCH, D_MODEL), 1.0),
        'ctx': nrm(ks[2], (BATCH, CTX_LEN, D_MODEL), 1.0),
        'c_ctx': nrm(ks[3], (D_MODEL,), 1.0),
        'w_mod': nrm(ks[4], (DEPTH, D_MODEL, 6 * D_MODEL), 0.5 * D_MODEL ** -0.5),
        'b_mod': nrm(ks[5], (DEPTH, 6 * D_MODEL), 0.02),
        'norm1_g': 1.0 + nrm(ks[6], (DEPTH, D_MODEL), 0.02),
        'norm2_g': 1.0 + nrm(ks[7], (DEPTH, D_MODEL), 0.02),
        'final_g': 1.0 + nrm(ks[8], (D_MODEL,), 0.02),
        'w_in': nrm(ks[9], (DEPTH, D_MODEL, IN_W), D_MODEL ** -0.5),
        'ml_conv_w': nrm(ks[12], (DEPTH, 3, 3, 2 * ML_QK_W), 1.0 / 3.0),
        'ml_gate_b': ml_gate_b,
        'ml_norm_g': 1.0 + nrm(ks[13], (DEPTH, ML_V_W), 0.02),
        'hy_conv_w': nrm(ks[14], (DEPTH, HY_SHORT, 3 * HY_W), HY_SHORT ** -0.5),
        'hy_w1': nrm(ks[15], (DEPTH, HY_FEAT, HY_FFN), HY_FEAT ** -0.5),
        'hy_b1': nrm(ks[16], (DEPTH, HY_FFN), 0.02),
        'hy_w2': nrm(ks[17], (DEPTH, HY_FFN, HY_FFN), HY_FFN ** -0.5),
        'hy_b2': nrm(ks[18], (DEPTH, HY_FFN), 0.02),
        'hy_w3': nrm(ks[19], (DEPTH, HY_FFN, 2 * HY_ORDER * HY_W), 0.05 * HY_FFN ** -0.5),
        'hy_freq': 1.0 + nrm(ks[20], (DEPTH, HY_FFN), 0.1),
        'hy_bias': nrm(ks[21], (DEPTH, HY_ORDER, HY_W), 0.5),
        'w_proj_ml': nrm(ks[22], (DEPTH, ML_V_W, D_MODEL), ML_V_W ** -0.5),
        'w_proj_hy': nrm(ks[23], (DEPTH, HY_W, D_MODEL), HY_W ** -0.5),
        'w_out': nrm(ks[24], (DEPTH, D_MODEL, D_MODEL), D_MODEL ** -0.5),
        'peer_wq': nrm(ks[25], (DEPTH, D_MODEL, PEER_HEADS * PEER_DKEY), D_MODEL ** -0.5),
        'peer_keys': nrm(ks[26], (DEPTH, PEER_HEADS, 2, PEER_NKEYS, PEER_DKEY // 2), (PEER_DKEY // 2) ** -0.5),
        'peer_u': nrm(ks[27], (DEPTH, PEER_N, D_MODEL), D_MODEL ** -0.5),
        'peer_v': nrm(ks[28], (DEPTH, PEER_N, D_MODEL), 0.5),
    }


def reference(x, c, ctx, c_ctx, w_mod, b_mod, norm1_g, norm2_g, final_g, w_in, ml_conv_w,
              ml_gate_b, ml_norm_g, hy_conv_w, hy_w1, hy_b1, hy_w2, hy_b2, hy_w3, hy_freq,
              hy_bias, w_proj_ml, w_proj_hy, w_out, peer_wq, peer_keys, peer_u, peer_v):
    B, L, D = x.shape
    rows = L // GRID_W
    L_ctx = ctx.shape[1]
    s_lat = jax.nn.silu(c)[:, None, :]
    s_ctx = jax.nn.silu(c_ctx)
    h_lat, h_ctx = x, ctx
    for l in range(DEPTH):
        last = l == DEPTH - 1
        sh1, sc1, g1, sh2, sc2, g2 = jnp.split(s_lat @ w_mod[l] + b_mod[l], 6, axis=-1)
        csh1, csc1, cg1, csh2, csc2, cg2 = jnp.split(s_ctx @ w_mod[l] + b_mod[l], 6, axis=-1)

        u_lat = rmsnorm(h_lat, norm1_g[l]) * (1.0 + sc1) + sh1
        u_ctx = rmsnorm(h_ctx, norm1_g[l]) * (1.0 + csc1) + csh1
        qk_l, v_l, o_l, gt_l, hy_l, bg_l = jnp.split(u_lat @ w_in[l], IN_SPLIT, axis=-1)
        qk_c, v_c, o_c, gt_c, hy_c, bg_c = jnp.split(u_ctx @ w_in[l], IN_SPLIT, axis=-1)

        qk_l = jax.nn.silu(dwconv_grid(qk_l, ml_conv_w[l], rows))
        qk_c = jax.nn.silu(dwconv1d(qk_c, ml_conv_w[l][1]))
        q_l, k_l, vh_l, gf_l, gb_l = mlstm_heads(qk_l, v_l, gt_l, ml_gate_b[l])
        q_c, k_c, vh_c, gf_c, gb_c = mlstm_heads(qk_c, v_c, gt_c, ml_gate_b[l])
        st_f = mlstm_final_state(k_c, vh_c, gf_c[0], gf_c[1])
        st_b = mlstm_final_state(_flip(k_c), _flip(vh_c), _flip(gb_c[0]), _flip(gb_c[1]))
        h_ml_l = mlstm_out(mlstm_bidir(q_l, k_l, vh_l, gf_l, gb_l, st_f, st_b), o_l, ml_norm_g[l])

        filt_l = hyena_filters(L, hy_w1[l], hy_b1[l], hy_w2[l], hy_b2[l], hy_w3[l], hy_freq[l])
        h_hy_l = hyena(dwconv1d(hy_l, hy_conv_w[l]), filt_l, hy_bias[l])
        mix_l = merge(h_ml_l, h_hy_l, bg_l, w_proj_ml[l], w_proj_hy[l], w_out[l])

        if not last:
            zero = mlstm_zero_state(B)
            h_ml_c = mlstm_out(mlstm_bidir(q_c, k_c, vh_c, gf_c, gb_c, zero, zero), o_c, ml_norm_g[l])
            filt_c = hyena_filters(L_ctx, hy_w1[l], hy_b1[l], hy_w2[l], hy_b2[l], hy_w3[l], hy_freq[l])
            h_hy_c = hyena(dwconv1d(hy_c, hy_conv_w[l]), filt_c, hy_bias[l])
            mix_c = merge(h_ml_c, h_hy_c, bg_c, w_proj_ml[l], w_proj_hy[l], w_out[l])

        h_lat = h_lat + g1 * mix_l

        v_lat = rmsnorm(h_lat, norm2_g[l]) * (1.0 + sc2) + sh2
        h_lat = h_lat + g2 * peer(v_lat, peer_wq[l], peer_keys[l], peer_u[l], peer_v[l])

        if not last:
            h_ctx = h_ctx + cg1 * mix_c
            v_ctx = rmsnorm(h_ctx, norm2_g[l]) * (1.0 + csc2) + csh2
            h_ctx = h_ctx + cg2 * peer(v_ctx, peer_wq[l], peer_keys[l], peer_u[l], peer_v[l])

    return rmsnorm(h_lat, final_g)
```

```python
import functools
import math
import jax
import jax.numpy as jnp
from jax import lax
from jax.experimental import pallas as pl
from jax.experimental.pallas import tpu as pltpu

D_MODEL = 2048
GRID_W = 64
EPS = 1e-6

ML_HEADS = 8
ML_DQK = D_MODEL // 16
ML_DV = D_MODEL // 8
ML_QK_W = ML_HEADS * ML_DQK
ML_V_W = ML_HEADS * ML_DV
ML_CHUNK = 64

HY_W = D_MODEL
HY_ORDER = 2
HY_BANDS = 8
HY_SHIFT = 0.05
HY_MIN_DECAY = math.log(1e-2) / 1.5
HY_MAX_DECAY = math.log(1e-2) / 0.3

PEER_HEADS = 8
PEER_NKEYS = 128
PEER_DKEY = 256
PEER_TOPK = 16
PEER_BLOCK = 128

IN_SPLIT = (2 * ML_QK_W,
            2 * ML_QK_W + ML_V_W,
            2 * ML_QK_W + 2 * ML_V_W,
            2 * ML_QK_W + 2 * ML_V_W + 4 * ML_HEADS,
            2 * ML_QK_W + 2 * ML_V_W + 4 * ML_HEADS + 3 * HY_W)

VMEM_LIMIT_BYTES = 56 * 1024 * 1024


def _mm_kernel(a_ref, b_ref, o_ref):
    o_ref[...] = jnp.dot(a_ref[...].astype(jnp.bfloat16), b_ref[...].astype(jnp.bfloat16),
                         preferred_element_type=jnp.float32).astype(o_ref.dtype)


def pmm(a, b, tm=512, tn=512, out_dtype=jnp.float32):
    M, K = a.shape
    _, N = b.shape
    tm = min(tm, M)
    tn = min(tn, N)
    assert M % tm == 0 and N % tn == 0, (M, N, tm, tn)
    return pl.pallas_call(
        _mm_kernel,
        grid=(M // tm, N // tn),
        in_specs=[pl.BlockSpec((tm, K), lambda i, j: (i, 0)),
                  pl.BlockSpec((K, tn), lambda i, j: (0, j))],
        out_specs=pl.BlockSpec((tm, tn), lambda i, j: (i, j)),
        out_shape=jax.ShapeDtypeStruct((M, N), out_dtype),
        compiler_params=pltpu.CompilerParams(
            dimension_semantics=("parallel", "parallel"),
            vmem_limit_bytes=VMEM_LIMIT_BYTES),
    )(a, b)


def rmsnorm(x, g):
    xf = x.astype(jnp.float32)
    y = xf * lax.rsqrt(jnp.mean(xf * xf, axis=-1, keepdims=True) + EPS)
    return y.astype(x.dtype) * g


def dwconv1d(x, w):
    K = w.shape[0]
    L = x.shape[1]
    pad = K // 2
    xp = jnp.pad(x, ((0, 0), (pad, pad), (0, 0)))
    y = xp[:, 0:L] * w[0]
    for j in range(1, K):
        y = y + xp[:, j:j + L] * w[j]
    return y


def dwconv_grid(x, w, rows):
    B, L, C = x.shape
    img = x.reshape(B, rows, GRID_W, C)
    y = lax.conv_general_dilated(img, w[:, :, None, :], (1, 1), 'SAME',
                                 dimension_numbers=('NHWC', 'HWIO', 'NHWC'),
                                 feature_group_count=C)
    return y.reshape(B, L, C)


def _flip(a):
    return jnp.flip(a, axis=2)


def mlstm_heads(qk, v, gates, gate_b):
    B, L, _ = v.shape
    q, k = jnp.split(qk, 2, axis=-1)
    q = q.reshape(B, L, ML_HEADS, ML_DQK).transpose(0, 2, 1, 3).astype(jnp.float32) * (ML_DQK ** -0.5)
    k = k.reshape(B, L, ML_HEADS, ML_DQK).transpose(0, 2, 1, 3).astype(jnp.float32)
    vh = v.reshape(B, L, ML_HEADS, ML_DV).transpose(0, 2, 1, 3).astype(jnp.float32)
    g = gates.astype(jnp.float32).reshape(B, L, 4, ML_HEADS).transpose(2, 0, 3, 1) \
        + gate_b.astype(jnp.float32)[:, None, :, None]
    g_fwd = (g[0], jax.nn.log_sigmoid(g[1]))
    g_bwd = (g[2], jax.nn.log_sigmoid(g[3]))
    return q, k, vh, g_fwd, g_bwd


def mlstm_final_state(k, v, log_i, log_f):
    b = jnp.cumsum(log_f, axis=-1)
    dec = b[..., -1:] - b + log_i
    m = jnp.max(dec, axis=-1)
    w = jnp.exp(dec - m[..., None])
    C = jnp.einsum('bhs,bhsd,bhsv->bhdv', w, k, v)
    n = jnp.einsum('bhs,bhsd->bhd', w, k)
    return (C, n, m)


def mlstm_chunkwise(q, k, v, log_i, log_f, state):
    B, H, L, _ = q.shape
    T = ML_CHUNK
    nc = L // T

    def to_chunks(a):
        return jnp.moveaxis(a.reshape(a.shape[:2] + (nc, T) + a.shape[3:]), 2, 0)

    causal = jnp.tril(jnp.ones((T, T), dtype=bool))

    def step(carry, inp):
        C, n, m = carry
        qb, kb, vb, ib, fb = inp
        b = jnp.cumsum(fb, axis=-1)
        D = b[..., :, None] - b[..., None, :] + ib[..., None, :]
        D = jnp.where(causal, D, -jnp.inf)
        inter = b + m[..., None]
        m_t = jnp.maximum(inter, jnp.max(D, axis=-1))
        w = jnp.exp(D - m_t[..., None])
        s_inter = jnp.exp(inter - m_t)
        qk = jnp.einsum('bhtd,bhsd->bhts', qb, kb) * w
        num = jnp.einsum('bhts,bhsv->bhtv', qk, vb) \
            + s_inter[..., None] * jnp.einsum('bhtd,bhdv->bhtv', qb, C)
        den = jnp.sum(qk, axis=-1) + s_inter * jnp.einsum('bhtd,bhd->bht', qb, n)
        h = num / jnp.maximum(jnp.abs(den), jnp.exp(-m_t))[..., None]
        bT = b[..., -1]
        dec = bT[..., None] - b + ib
        m_new = jnp.maximum(bT + m, jnp.max(dec, axis=-1))
        wk = jnp.exp(dec - m_new[..., None])
        s_old = jnp.exp(bT + m - m_new)
        C_new = s_old[..., None, None] * C + jnp.einsum('bhs,bhsd,bhsv->bhdv', wk, kb, vb)
        n_new = s_old[..., None] * n + jnp.einsum('bhs,bhsd->bhd', wk, kb)
        return (C_new, n_new, m_new), h

    _, h = lax.scan(step, state, (to_chunks(q), to_chunks(k), to_chunks(v),
                                  to_chunks(log_i), to_chunks(log_f)))
    return jnp.moveaxis(h, 0, 2).reshape(B, H, L, v.shape[-1])


def mlstm_bidir(q, k, v, g_fwd, g_bwd, st_f, st_b):
    h_f = mlstm_chunkwise(q, k, v, g_fwd[0], g_fwd[1], st_f)
    h_b = mlstm_chunkwise(_flip(q), _flip(k), _flip(v), _flip(g_bwd[0]), _flip(g_bwd[1]), st_b)
    return h_f + _flip(h_b)


def mlstm_out(h, o, g):
    B, H, L, dv = h.shape
    h = h * lax.rsqrt(jnp.mean(h * h, axis=-1, keepdims=True) + EPS)
    h = h.transpose(0, 2, 1, 3).reshape(B, L, H * dv).astype(o.dtype)
    return h * g * jax.nn.sigmoid(o)


def hyena_filters(L, w1, b1, w2, b2, w3, freq):
    t = jnp.arange(L, dtype=jnp.float32)
    tn = t / L
    bands = jnp.linspace(1e-4, HY_BANDS - 1, HY_BANDS, dtype=jnp.float32)
    ang = (2.0 * math.pi / L) * t[:, None] * bands[None, :]
    feats = jnp.concatenate([tn[:, None], jnp.cos(ang), -jnp.sin(ang)], axis=-1)
    hdn = jnp.sin(freq * (feats @ w1 + b1))
    hdn = jnp.sin(freq * (hdn @ w2 + b2))
    filt = (hdn @ w3).astype(jnp.float32).reshape(L, 2, HY_ORDER, HY_W)
    deltas = jnp.abs(jnp.linspace(HY_MIN_DECAY, HY_MAX_DECAY, HY_W, dtype=jnp.float32))
    window = jnp.exp(-tn[:, None] * deltas[None, :]) + HY_SHIFT
    return filt * window[:, None, None, :]


def long_conv(z, h_fwd, h_bwd, bias):
    L = z.shape[1]
    n = 2 * L
    Z = jnp.fft.rfft(z.astype(jnp.float32), n=n, axis=1)
    Hf = jnp.fft.rfft(h_fwd, n=n, axis=0)
    Hb = jnp.fft.rfft(h_bwd, n=n, axis=0)
    y = jnp.fft.irfft(Z * (Hf + jnp.conj(Hb))[None], n=n, axis=1)[:, :L]
    return (y + z.astype(jnp.float32) * bias.astype(jnp.float32)).astype(z.dtype)


def hyena(xs, filt, bias):
    x1, x2, z = jnp.split(xs, 3, axis=-1)
    z = x1 * long_conv(z, filt[:, 0, 0], filt[:, 1, 0], bias[0])
    z = x2 * long_conv(z, filt[:, 0, 1], filt[:, 1, 1], bias[1])
    return z


def merge(h_ml, h_hy, bg, w_pm, w_ph, w_o):
    g_ml, g_hy = jnp.split(bg, 2, axis=-1)
    y = jax.nn.sigmoid(g_ml) * pmm(h_ml[0], w_pm)[None] + jax.nn.sigmoid(g_hy) * pmm(h_hy[0], w_ph)[None]
    return pmm(y[0], w_o)[None]


def peer(u, w_q, sub_keys, expert_u, expert_v):
    B, L, D = u.shape
    qall = pmm(u.reshape(B * L, D), w_q)
    blocks = u.reshape(B * L // PEER_BLOCK, PEER_BLOCK, D)
    qblocks = qall.reshape(B * L // PEER_BLOCK, PEER_BLOCK, PEER_HEADS * PEER_DKEY)

    def one_block(args):
        ub, qb = args
        q = qb.reshape(PEER_BLOCK, PEER_HEADS, 2, PEER_DKEY // 2)
        s = jnp.einsum('thpd,hpkd->thpk', q, sub_keys).astype(jnp.float32)
        s1, i1 = lax.top_k(s[:, :, 0], PEER_TOPK)
        s2, i2 = lax.top_k(s[:, :, 1], PEER_TOPK)
        cand = (s1[..., :, None] + s2[..., None, :]).reshape(PEER_BLOCK, PEER_HEADS, PEER_TOPK * PEER_TOPK)
        cand_idx = (i1[..., :, None] * PEER_NKEYS + i2[..., None, :]).reshape(PEER_BLOCK, PEER_HEADS, PEER_TOPK * PEER_TOPK)
        best, pos = lax.top_k(cand, PEER_TOPK)
        e = jnp.take_along_axis(cand_idx, pos, axis=-1)
        g = jax.nn.softmax(best, axis=-1).astype(ub.dtype)
        act = jax.nn.gelu(jnp.einsum('td,thkd->thk', ub, expert_u[e]), approximate=False) * g
        return jnp.einsum('thk,thkd->td', act, expert_v[e])

    return lax.map(one_block, (blocks, qblocks)).reshape(B, L, D)


def kernel(x, c, ctx, c_ctx, w_mod, b_mod, norm1_g, norm2_g, final_g, w_in, ml_conv_w, ml_gate_b, ml_norm_g, hy_conv_w, hy_w1, hy_b1, hy_w2, hy_b2, hy_w3, hy_freq, hy_bias, w_proj_ml, w_proj_hy, w_out, peer_wq, peer_keys, peer_u, peer_v):
    B, L, D = x.shape
    rows = L // GRID_W
    l = 0
    s_lat = jax.nn.silu(c)[:, None, :]
    s_ctx = jax.nn.silu(c_ctx)
    h_lat, h_ctx = x, ctx
    sh1, sc1, g1, sh2, sc2, g2 = jnp.split(s_lat @ w_mod[l] + b_mod[l], 6, axis=-1)
    csh1, csc1, cg1, csh2, csc2, cg2 = jnp.split(s_ctx @ w_mod[l] + b_mod[l], 6, axis=-1)

    u_lat = rmsnorm(h_lat, norm1_g[l]) * (1.0 + sc1) + sh1
    u_ctx = rmsnorm(h_ctx, norm1_g[l]) * (1.0 + csc1) + csh1
    w_in_p = jnp.pad(w_in[l], ((0, 0), (0, 16896 - w_in.shape[-1])))
    p_lat = pmm(u_lat[0], w_in_p)[None, :, :w_in.shape[-1]]
    p_ctx = pmm(u_ctx[0], w_in_p)[None, :, :w_in.shape[-1]]
    qk_l, v_l, o_l, gt_l, hy_l, bg_l = jnp.split(p_lat, IN_SPLIT, axis=-1)
    qk_c, v_c, o_c, gt_c, hy_c, bg_c = jnp.split(p_ctx, IN_SPLIT, axis=-1)

    qk_l = jax.nn.silu(dwconv_grid(qk_l, ml_conv_w[l], rows))
    qk_c = jax.nn.silu(dwconv1d(qk_c, ml_conv_w[l][1]))
    q_l, k_l, vh_l, gf_l, gb_l = mlstm_heads(qk_l, v_l, gt_l, ml_gate_b[l])
    q_c, k_c, vh_c, gf_c, gb_c = mlstm_heads(qk_c, v_c, gt_c, ml_gate_b[l])
    st_f = mlstm_final_state(k_c, vh_c, gf_c[0], gf_c[1])
    st_b = mlstm_final_state(_flip(k_c), _flip(vh_c), _flip(gb_c[0]), _flip(gb_c[1]))
    h_ml_l = mlstm_out(mlstm_bidir(q_l, k_l, vh_l, gf_l, gb_l, st_f, st_b), o_l, ml_norm_g[l])

    filt_l = hyena_filters(L, hy_w1[l], hy_b1[l], hy_w2[l], hy_b2[l], hy_w3[l], hy_freq[l])
    h_hy_l = hyena(dwconv1d(hy_l, hy_conv_w[l]), filt_l, hy_bias[l])
    mix_l = merge(h_ml_l, h_hy_l, bg_l, w_proj_ml[l], w_proj_hy[l], w_out[l])

    h_lat = h_lat + g1 * mix_l
    v_lat = rmsnorm(h_lat, norm2_g[l]) * (1.0 + sc2) + sh2
    h_lat = h_lat + g2 * peer(v_lat, peer_wq[l], peer_keys[l], peer_u[l], peer_v[l])
    return rmsnorm(h_lat, final_g)
```

```python
import functools
import math
import jax
import jax.numpy as jnp
from jax import lax
from jax.experimental import pallas as pl
from jax.experimental.pallas import tpu as pltpu

D_MODEL = 2048
GRID_W = 64
EPS = 1e-6

ML_HEADS = 8
ML_DQK = D_MODEL // 16
ML_DV = D_MODEL // 8
ML_QK_W = ML_HEADS * ML_DQK
ML_V_W = ML_HEADS * ML_DV
ML_CHUNK = 64

HY_W = D_MODEL
HY_ORDER = 2
HY_BANDS = 8
HY_SHIFT = 0.05
HY_MIN_DECAY = math.log(1e-2) / 1.5
HY_MAX_DECAY = math.log(1e-2) / 0.3

PEER_HEADS = 8
PEER_NKEYS = 128
PEER_DKEY = 256
PEER_TOPK = 16
PEER_BLOCK = 128

IN_SPLIT = (2 * ML_QK_W,
            2 * ML_QK_W + ML_V_W,
            2 * ML_QK_W + 2 * ML_V_W,
            2 * ML_QK_W + 2 * ML_V_W + 4 * ML_HEADS,
            2 * ML_QK_W + 2 * ML_V_W + 4 * ML_HEADS + 3 * HY_W)

VMEM_LIMIT_BYTES = 56 * 1024 * 1024


def _mm_kernel(a_ref, b_ref, o_ref):
    o_ref[...] = jnp.dot(a_ref[...].astype(jnp.bfloat16), b_ref[...].astype(jnp.bfloat16),
                         preferred_element_type=jnp.float32).astype(o_ref.dtype)


def pmm(a, b, tm=512, tn=512, out_dtype=jnp.float32):
    M, K = a.shape
    _, N = b.shape
    tm = min(tm, M)
    tn = min(tn, N)
    assert M % tm == 0 and N % tn == 0, (M, N, tm, tn)
    return pl.pallas_call(
        _mm_kernel,
        grid=(M // tm, N // tn),
        in_specs=[pl.BlockSpec((tm, K), lambda i, j: (i, 0)),
                  pl.BlockSpec((K, tn), lambda i, j: (0, j))],
        out_specs=pl.BlockSpec((tm, tn), lambda i, j: (i, j)),
        out_shape=jax.ShapeDtypeStruct((M, N), out_dtype),
        compiler_params=pltpu.CompilerParams(
            dimension_semantics=("parallel", "parallel"),
            vmem_limit_bytes=VMEM_LIMIT_BYTES),
    )(a, b)


def rmsnorm(x, g):
    xf = x.astype(jnp.float32)
    y = xf * lax.rsqrt(jnp.mean(xf * xf, axis=-1, keepdims=True) + EPS)
    return y.astype(x.dtype) * g


def dwconv1d(x, w):
    K = w.shape[0]
    L = x.shape[1]
    pad = K // 2
    xp = jnp.pad(x, ((0, 0), (pad, pad), (0, 0)))
    y = xp[:, 0:L] * w[0]
    for j in range(1, K):
        y = y + xp[:, j:j + L] * w[j]
    return y


def dwconv_grid(x, w, rows):
    B, L, C = x.shape
    img = x.reshape(B, rows, GRID_W, C)
    y = lax.conv_general_dilated(img, w[:, :, None, :], (1, 1), 'SAME',
                                 dimension_numbers=('NHWC', 'HWIO', 'NHWC'),
                                 feature_group_count=C)
    return y.reshape(B, L, C)


def _flip(a):
    return jnp.flip(a, axis=2)


def mlstm_heads(qk, v, gates, gate_b):
    B, L, _ = v.shape
    q, k = jnp.split(qk, 2, axis=-1)
    q = q.reshape(B, L, ML_HEADS, ML_DQK).transpose(0, 2, 1, 3).astype(jnp.float32) * (ML_DQK ** -0.5)
    k = k.reshape(B, L, ML_HEADS, ML_DQK).transpose(0, 2, 1, 3).astype(jnp.float32)
    vh = v.reshape(B, L, ML_HEADS, ML_DV).transpose(0, 2, 1, 3).astype(jnp.float32)
    g = gates.astype(jnp.float32).reshape(B, L, 4, ML_HEADS).transpose(2, 0, 3, 1) \
        + gate_b.astype(jnp.float32)[:, None, :, None]
    g_fwd = (g[0], jax.nn.log_sigmoid(g[1]))
    g_bwd = (g[2], jax.nn.log_sigmoid(g[3]))
    return q, k, vh, g_fwd, g_bwd


def mlstm_final_state(k, v, log_i, log_f):
    b = jnp.cumsum(log_f, axis=-1)
    dec = b[..., -1:] - b + log_i
    m = jnp.max(dec, axis=-1)
    w = jnp.exp(dec - m[..., None])
    C = jnp.einsum('bhs,bhsd,bhsv->bhdv', w, k, v)
    n = jnp.einsum('bhs,bhsd->bhd', w, k)
    return (C, n, m)


def mlstm_chunkwise(q, k, v, log_i, log_f, state):
    B, H, L, _ = q.shape
    T = ML_CHUNK
    nc = L // T

    def to_chunks(a):
        return jnp.moveaxis(a.reshape(a.shape[:2] + (nc, T) + a.shape[3:]), 2, 0)

    causal = jnp.tril(jnp.ones((T, T), dtype=bool))

    def step(carry, inp):
        C, n, m = carry
        qb, kb, vb, ib, fb = inp
        b = jnp.cumsum(fb, axis=-1)
        D = b[..., :, None] - b[..., None, :] + ib[..., None, :]
        D = jnp.where(causal, D, -jnp.inf)
        inter = b + m[..., None]
        m_t = jnp.maximum(inter, jnp.max(D, axis=-1))
        w = jnp.exp(D - m_t[..., None])
        s_inter = jnp.exp(inter - m_t)
        qk = jnp.einsum('bhtd,bhsd->bhts', qb, kb) * w
        num = jnp.einsum('bhts,bhsv->bhtv', qk, vb) \
            + s_inter[..., None] * jnp.einsum('bhtd,bhdv->bhtv', qb, C)
        den = jnp.sum(qk, axis=-1) + s_inter * jnp.einsum('bhtd,bhd->bht', qb, n)
        h = num / jnp.maximum(jnp.abs(den), jnp.exp(-m_t))[..., None]
        bT = b[..., -1]
        dec = bT[..., None] - b + ib
        m_new = jnp.maximum(bT + m, jnp.max(dec, axis=-1))
        wk = jnp.exp(dec - m_new[..., None])
        s_old = jnp.exp(bT + m - m_new)
        C_new = s_old[..., None, None] * C + jnp.einsum('bhs,bhsd,bhsv->bhdv', wk, kb, vb)
        n_new = s_old[..., None] * n + jnp.einsum('bhs,bhsd->bhd', wk, kb)
        return (C_new, n_new, m_new), h

    _, h = lax.scan(step, state, (to_chunks(q), to_chunks(k), to_chunks(v),
                                  to_chunks(log_i), to_chunks(log_f)))
    return jnp.moveaxis(h, 0, 2).reshape(B, H, L, v.shape[-1])


def mlstm_bidir(q, k, v, g_fwd, g_bwd, st_f, st_b):
    h_f = mlstm_chunkwise(q, k, v, g_fwd[0], g_fwd[1], st_f)
    h_b = mlstm_chunkwise(_flip(q), _flip(k), _flip(v), _flip(g_bwd[0]), _flip(g_bwd[1]), st_b)
    return h_f + _flip(h_b)


def mlstm_out(h, o, g):
    B, H, L, dv = h.shape
    h = h * lax.rsqrt(jnp.mean(h * h, axis=-1, keepdims=True) + EPS)
    h = h.transpose(0, 2, 1, 3).reshape(B, L, H * dv).astype(o.dtype)
    return h * g * jax.nn.sigmoid(o)


def hyena_filters(L, w1, b1, w2, b2, w3, freq):
    t = jnp.arange(L, dtype=jnp.float32)
    tn = t / L
    bands = jnp.linspace(1e-4, HY_BANDS - 1, HY_BANDS, dtype=jnp.float32)
    ang = (2.0 * math.pi / L) * t[:, None] * bands[None, :]
    feats = jnp.concatenate([tn[:, None], jnp.cos(ang), -jnp.sin(ang)], axis=-1)
    hdn = jnp.sin(freq * (feats @ w1 + b1))
    hdn = jnp.sin(freq * (hdn @ w2 + b2))
    filt = (hdn @ w3).astype(jnp.float32).reshape(L, 2, HY_ORDER, HY_W)
    deltas = jnp.abs(jnp.linspace(HY_MIN_DECAY, HY_MAX_DECAY, HY_W, dtype=jnp.float32))
    window = jnp.exp(-tn[:, None] * deltas[None, :]) + HY_SHIFT
    return filt * window[:, None, None, :]


def long_conv(z, h_fwd, h_bwd, bias):
    L = z.shape[1]
    n = 2 * L
    Z = jnp.fft.rfft(z.astype(jnp.float32), n=n, axis=1)
    Hf = jnp.fft.rfft(h_fwd, n=n, axis=0)
    Hb = jnp.fft.rfft(h_bwd, n=n, axis=0)
    y = jnp.fft.irfft(Z * (Hf + jnp.conj(Hb))[None], n=n, axis=1)[:, :L]
    return (y + z.astype(jnp.float32) * bias.astype(jnp.float32)).astype(z.dtype)


def hyena(xs, filt, bias):
    x1, x2, z = jnp.split(xs, 3, axis=-1)
    z = x1 * long_conv(z, filt[:, 0, 0], filt[:, 1, 0], bias[0])
    z = x2 * long_conv(z, filt[:, 0, 1], filt[:, 1, 1], bias[1])
    return z


def merge(h_ml, h_hy, bg, w_pm, w_ph, w_o):
    g_ml, g_hy = jnp.split(bg, 2, axis=-1)
    y = jax.nn.sigmoid(g_ml) * pmm(h_ml[0], w_pm)[None] + jax.nn.sigmoid(g_hy) * pmm(h_hy[0], w_ph)[None]
    return pmm(y[0], w_o)[None]


LANES = 128
NEG_INF = float('-inf')
_CAND_ROWS = tuple(PEER_TOPK // (i + 1) for i in range(PEER_TOPK))


def _extract_top(s, rounds):
    rows = s.shape[0]
    iota = lax.broadcasted_iota(jnp.int32, s.shape, 0)
    tops = []
    for _ in range(rounds):
        m = jnp.max(s, axis=0, keepdims=True)
        first = jnp.min(jnp.where(s == m, iota, rows), axis=0, keepdims=True)
        s = jnp.where(iota == first, NEG_INF, s)
        tops.append(m)
    return tops


def _peer_topk_kernel(q_ref, k_ref, s1_ref, s2_ref, e1_ref, e2_ref, thr_ref):
    tk = q_ref.shape[0]
    for u in range(tk // LANES):
        tok = slice(u * LANES, (u + 1) * LANES)
        s = []
        tops = []
        for p in range(2):
            qp = q_ref[tok, p * PEER_NKEYS:(p + 1) * PEER_NKEYS]
            sp = lax.dot_general(k_ref[0, p], qp, (((1,), (1,)), ((), ())),
                                 preferred_element_type=jnp.float32)
            s.append(sp)
            tops.append(_extract_top(sp, PEER_TOPK))
        top2a = jnp.concatenate(tops[1][:8], axis=0)
        top2b = jnp.concatenate(tops[1][8:], axis=0)
        row = lax.broadcasted_iota(jnp.int32, (8, LANES), 0)
        cands = [tops[0][0] + top2a, tops[0][0] + top2b]
        for i in range(1, PEER_TOPK):
            cands.append(jnp.where(row < _CAND_ROWS[i], tops[0][i] + top2a, NEG_INF))
        best = _extract_top(jnp.concatenate(cands, axis=0), PEER_TOPK)
        z = jnp.ones_like(best[0])
        for b in best[1:]:
            z = z + jnp.exp(b - best[0])
        s1_ref[0, :, tok] = s[0]
        s2_ref[0, :, tok] = s[1]
        e1_ref[0, :, tok] = jnp.exp(s[0] - tops[0][0]) / z
        e2_ref[0, :, tok] = jnp.exp(s[1] - tops[1][0])
        thr_ref[0, :, tok] = best[-1]


def peer_topk(q, sub_keys, tk=256):
    T = q.shape[0]
    H = PEER_HEADS
    big = jax.ShapeDtypeStruct((H, PEER_NKEYS, T), jnp.float32)
    big_spec = pl.BlockSpec((1, PEER_NKEYS, tk), lambda i, h: (h, 0, i))
    s1, s2, e1, e2, thr = pl.pallas_call(
        _peer_topk_kernel,
        grid=(T // tk, H),
        in_specs=[pl.BlockSpec((tk, PEER_DKEY), lambda i, h: (i, h)),
                  pl.BlockSpec((1, 2, PEER_NKEYS, PEER_DKEY // 2), lambda i, h: (h, 0, 0, 0))],
        out_specs=[big_spec, big_spec, big_spec, big_spec,
                   pl.BlockSpec((1, 1, tk), lambda i, h: (h, 0, i))],
        out_shape=[big, big, big, big, jax.ShapeDtypeStruct((H, 1, T), jnp.float32)],
        compiler_params=pltpu.CompilerParams(
            dimension_semantics=("parallel", "parallel"),
            vmem_limit_bytes=VMEM_LIMIT_BYTES),
    )(q, sub_keys.astype(jnp.bfloat16))
    return s1, s2, e1, e2, thr.reshape(H, T)


def _peer_expert_kernel(v_ref, u_ref, vt_ref, s1a_ref, e1a_ref, s2_ref, e2_ref, thr_ref, o_ref, sc_ref, w_ref):
    j = pl.program_id(1)
    tt = v_ref.shape[0]
    na = u_ref.shape[0] // PEER_NKEYS

    @pl.when(j == 0)
    def _():
        o_ref[...] = jnp.zeros_like(o_ref)

    sc_ref[...] = lax.dot_general(u_ref[...], v_ref[...], (((1,), (1,)), ((), ())),
                                  preferred_element_type=jnp.float32)

    for u in range(tt // LANES):
        tok = slice(u * LANES, (u + 1) * LANES)

        def per_a(al, carry):
            g = jnp.zeros((PEER_NKEYS, LANES), jnp.float32)
            for h in range(PEER_HEADS):
                s1row = s1a_ref[al, h:h + 1, tok]
                e1row = e1a_ref[al, h:h + 1, tok]
                keep = (s1row + s2_ref[h, :, tok]) >= thr_ref[h:h + 1, tok]
                g = g + jnp.where(keep, e1row * e2_ref[h, :, tok], 0.0)
            rows = pl.ds(pl.multiple_of(al * PEER_NKEYS, PEER_NKEYS), PEER_NKEYS)
            sc = sc_ref[rows, tok]
            act = 0.5 * sc * (1.0 + lax.erf(sc * math.sqrt(0.5)))
            w_ref[rows, tok] = (act * g).astype(w_ref.dtype)
            return carry

        lax.fori_loop(0, na, per_a, 0)

    o_ref[...] += jnp.dot(vt_ref[...], w_ref[...], preferred_element_type=jnp.float32)


def peer_experts(v, expert_u, expert_vt, s1a, e1a, s2, e2, thr, tt=512, nb=1024):
    T, D = v.shape
    N = expert_u.shape[0]
    na = nb // PEER_NKEYS
    H = PEER_HEADS
    return pl.pallas_call(
        _peer_expert_kernel,
        grid=(T // tt, N // nb),
        in_specs=[pl.BlockSpec((tt, D), lambda i, j: (i, 0)),
                  pl.BlockSpec((nb, D), lambda i, j: (j, 0)),
                  pl.BlockSpec((D, nb), lambda i, j: (0, j)),
                  pl.BlockSpec((na, H, tt), lambda i, j: (j, 0, i)),
                  pl.BlockSpec((na, H, tt), lambda i, j: (j, 0, i)),
                  pl.BlockSpec((H, PEER_NKEYS, tt), lambda i, j: (0, 0, i)),
                  pl.BlockSpec((H, PEER_NKEYS, tt), lambda i, j: (0, 0, i)),
                  pl.BlockSpec((H, tt), lambda i, j: (0, i))],
        out_specs=pl.BlockSpec((D, tt), lambda i, j: (0, i)),
        out_shape=jax.ShapeDtypeStruct((D, T), jnp.float32),
        scratch_shapes=[pltpu.VMEM((nb, tt), jnp.float32), pltpu.VMEM((nb, tt), jnp.bfloat16)],
        compiler_params=pltpu.CompilerParams(
            dimension_semantics=("parallel", "arbitrary"),
            vmem_limit_bytes=VMEM_LIMIT_BYTES),
    )(v, expert_u, expert_vt, s1a, e1a, s2, e2, thr)


def peer(u, w_q, sub_keys, expert_u, expert_v, tk=256, tt=512, nb=1024):
    ub = u.astype(jnp.bfloat16)
    q = pmm(ub, w_q.astype(jnp.bfloat16), out_dtype=jnp.bfloat16)
    s1, s2, e1, e2, thr = peer_topk(q, sub_keys, tk=tk)
    out_t = peer_experts(ub, expert_u.astype(jnp.bfloat16), expert_v.astype(jnp.bfloat16).T,
                         s1.transpose(1, 0, 2), e1.transpose(1, 0, 2), s2, e2, thr, tt=tt, nb=nb)
    return out_t.T


def kernel(x, c, ctx, c_ctx, w_mod, b_mod, norm1_g, norm2_g, final_g, w_in, ml_conv_w, ml_gate_b, ml_norm_g, hy_conv_w, hy_w1, hy_b1, hy_w2, hy_b2, hy_w3, hy_freq, hy_bias, w_proj_ml, w_proj_hy, w_out, peer_wq, peer_keys, peer_u, peer_v):
    B, L, D = x.shape
    rows = L // GRID_W
    l = 0
    s_lat = jax.nn.silu(c)[:, None, :]
    s_ctx = jax.nn.silu(c_ctx)
    h_lat, h_ctx = x, ctx
    sh1, sc1, g1, sh2, sc2, g2 = jnp.split(s_lat @ w_mod[l] + b_mod[l], 6, axis=-1)
    csh1, csc1, cg1, csh2, csc2, cg2 = jnp.split(s_ctx @ w_mod[l] + b_mod[l], 6, axis=-1)

    u_lat = rmsnorm(h_lat, norm1_g[l]) * (1.0 + sc1) + sh1
    u_ctx = rmsnorm(h_ctx, norm1_g[l]) * (1.0 + csc1) + csh1
    w_in_p = jnp.pad(w_in[l], ((0, 0), (0, 16896 - w_in.shape[-1])))
    p_lat = pmm(u_lat[0], w_in_p)[None, :, :w_in.shape[-1]]
    p_ctx = pmm(u_ctx[0], w_in_p)[None, :, :w_in.shape[-1]]
    qk_l, v_l, o_l, gt_l, hy_l, bg_l = jnp.split(p_lat, IN_SPLIT, axis=-1)
    qk_c, v_c, o_c, gt_c, hy_c, bg_c = jnp.split(p_ctx, IN_SPLIT, axis=-1)

    qk_l = jax.nn.silu(dwconv_grid(qk_l, ml_conv_w[l], rows))
    qk_c = jax.nn.silu(dwconv1d(qk_c, ml_conv_w[l][1]))
    q_l, k_l, vh_l, gf_l, gb_l = mlstm_heads(qk_l, v_l, gt_l, ml_gate_b[l])
    q_c, k_c, vh_c, gf_c, gb_c = mlstm_heads(qk_c, v_c, gt_c, ml_gate_b[l])
    st_f = mlstm_final_state(k_c, vh_c, gf_c[0], gf_c[1])
    st_b = mlstm_final_state(_flip(k_c), _flip(vh_c), _flip(gb_c[0]), _flip(gb_c[1]))
    h_ml_l = mlstm_out(mlstm_bidir(q_l, k_l, vh_l, gf_l, gb_l, st_f, st_b), o_l, ml_norm_g[l])

    filt_l = hyena_filters(L, hy_w1[l], hy_b1[l], hy_w2[l], hy_b2[l], hy_w3[l], hy_freq[l])
    h_hy_l = hyena(dwconv1d(hy_l, hy_conv_w[l]), filt_l, hy_bias[l])
    mix_l = merge(h_ml_l, h_hy_l, bg_l, w_proj_ml[l], w_proj_hy[l], w_out[l])

    h_lat = h_lat + g1 * mix_l
    v_lat = rmsnorm(h_lat, norm2_g[l]) * (1.0 + sc2) + sh2
    h_lat = h_lat + g2 * peer(v_lat[0], peer_wq[l], peer_keys[l], peer_u[l], peer_v[l])[None]
    return rmsnorm(h_lat, final_g)
```

```python
import functools
import math
import numpy as np
import jax
import jax.numpy as jnp
from jax import lax
from jax.experimental import pallas as pl
from jax.experimental.pallas import tpu as pltpu

D_MODEL = 2048
GRID_W = 64
EPS = 1e-6

ML_HEADS = 8
ML_DQK = D_MODEL // 16
ML_DV = D_MODEL // 8
ML_QK_W = ML_HEADS * ML_DQK
ML_V_W = ML_HEADS * ML_DV
ML_CHUNK = 64

HY_W = D_MODEL
HY_ORDER = 2
HY_BANDS = 8
HY_SHIFT = 0.05
HY_MIN_DECAY = math.log(1e-2) / 1.5
HY_MAX_DECAY = math.log(1e-2) / 0.3

PEER_HEADS = 8
PEER_NKEYS = 128
PEER_DKEY = 256
PEER_TOPK = 16
PEER_BLOCK = 128

IN_SPLIT = (2 * ML_QK_W,
            2 * ML_QK_W + ML_V_W,
            2 * ML_QK_W + 2 * ML_V_W,
            2 * ML_QK_W + 2 * ML_V_W + 4 * ML_HEADS,
            2 * ML_QK_W + 2 * ML_V_W + 4 * ML_HEADS + 3 * HY_W)

VMEM_LIMIT_BYTES = 56 * 1024 * 1024


def _mm_kernel(a_ref, b_ref, o_ref):
    o_ref[...] = jnp.dot(a_ref[...].astype(jnp.bfloat16), b_ref[...].astype(jnp.bfloat16),
                         preferred_element_type=jnp.float32).astype(o_ref.dtype)


def pmm(a, b, tm=512, tn=512, out_dtype=jnp.float32):
    M, K = a.shape
    _, N = b.shape
    tm = min(tm, M)
    tn = min(tn, N)
    assert M % tm == 0 and N % tn == 0, (M, N, tm, tn)
    return pl.pallas_call(
        _mm_kernel,
        grid=(M // tm, N // tn),
        in_specs=[pl.BlockSpec((tm, K), lambda i, j: (i, 0)),
                  pl.BlockSpec((K, tn), lambda i, j: (0, j))],
        out_specs=pl.BlockSpec((tm, tn), lambda i, j: (i, j)),
        out_shape=jax.ShapeDtypeStruct((M, N), out_dtype),
        compiler_params=pltpu.CompilerParams(
            dimension_semantics=("parallel", "parallel"),
            vmem_limit_bytes=VMEM_LIMIT_BYTES),
    )(a, b)


def rmsnorm(x, g):
    xf = x.astype(jnp.float32)
    y = xf * lax.rsqrt(jnp.mean(xf * xf, axis=-1, keepdims=True) + EPS)
    return y.astype(x.dtype) * g


def dwconv1d(x, w):
    K = w.shape[0]
    L = x.shape[1]
    pad = K // 2
    xp = jnp.pad(x, ((0, 0), (pad, pad), (0, 0)))
    y = xp[:, 0:L] * w[0]
    for j in range(1, K):
        y = y + xp[:, j:j + L] * w[j]
    return y


def dwconv_grid(x, w, rows):
    B, L, C = x.shape
    img = x.reshape(B, rows, GRID_W, C)
    y = lax.conv_general_dilated(img, w[:, :, None, :], (1, 1), 'SAME',
                                 dimension_numbers=('NHWC', 'HWIO', 'NHWC'),
                                 feature_group_count=C)
    return y.reshape(B, L, C)


def _flip(a):
    return jnp.flip(a, axis=2)


def mlstm_heads(qk, v, gates, gate_b):
    B, L, _ = v.shape
    q, k = jnp.split(qk, 2, axis=-1)
    q = q.reshape(B, L, ML_HEADS, ML_DQK).transpose(0, 2, 1, 3).astype(jnp.float32) * (ML_DQK ** -0.5)
    k = k.reshape(B, L, ML_HEADS, ML_DQK).transpose(0, 2, 1, 3).astype(jnp.float32)
    vh = v.reshape(B, L, ML_HEADS, ML_DV).transpose(0, 2, 1, 3).astype(jnp.float32)
    g = gates.astype(jnp.float32).reshape(B, L, 4, ML_HEADS).transpose(2, 0, 3, 1) \
        + gate_b.astype(jnp.float32)[:, None, :, None]
    g_fwd = (g[0], jax.nn.log_sigmoid(g[1]))
    g_bwd = (g[2], jax.nn.log_sigmoid(g[3]))
    return q, k, vh, g_fwd, g_bwd


def mlstm_final_state(k, v, log_i, log_f):
    b = jnp.cumsum(log_f, axis=-1)
    dec = b[..., -1:] - b + log_i
    m = jnp.max(dec, axis=-1)
    w = jnp.exp(dec - m[..., None])
    C = jnp.einsum('bhs,bhsd,bhsv->bhdv', w, k, v)
    n = jnp.einsum('bhs,bhsd->bhd', w, k)
    return (C, n, m)


def mlstm_chunkwise(q, k, v, log_i, log_f, state):
    B, H, L, _ = q.shape
    T = ML_CHUNK
    nc = L // T

    def to_chunks(a):
        return jnp.moveaxis(a.reshape(a.shape[:2] + (nc, T) + a.shape[3:]), 2, 0)

    causal = jnp.tril(jnp.ones((T, T), dtype=bool))

    def step(carry, inp):
        C, n, m = carry
        qb, kb, vb, ib, fb = inp
        b = jnp.cumsum(fb, axis=-1)
        D = b[..., :, None] - b[..., None, :] + ib[..., None, :]
        D = jnp.where(causal, D, -jnp.inf)
        inter = b + m[..., None]
        m_t = jnp.maximum(inter, jnp.max(D, axis=-1))
        w = jnp.exp(D - m_t[..., None])
        s_inter = jnp.exp(inter - m_t)
        qk = jnp.einsum('bhtd,bhsd->bhts', qb, kb) * w
        num = jnp.einsum('bhts,bhsv->bhtv', qk, vb) \
            + s_inter[..., None] * jnp.einsum('bhtd,bhdv->bhtv', qb, C)
        den = jnp.sum(qk, axis=-1) + s_inter * jnp.einsum('bhtd,bhd->bht', qb, n)
        h = num / jnp.maximum(jnp.abs(den), jnp.exp(-m_t))[..., None]
        bT = b[..., -1]
        dec = bT[..., None] - b + ib
        m_new = jnp.maximum(bT + m, jnp.max(dec, axis=-1))
        wk = jnp.exp(dec - m_new[..., None])
        s_old = jnp.exp(bT + m - m_new)
        C_new = s_old[..., None, None] * C + jnp.einsum('bhs,bhsd,bhsv->bhdv', wk, kb, vb)
        n_new = s_old[..., None] * n + jnp.einsum('bhs,bhsd->bhd', wk, kb)
        return (C_new, n_new, m_new), h

    _, h = lax.scan(step, state, (to_chunks(q), to_chunks(k), to_chunks(v),
                                  to_chunks(log_i), to_chunks(log_f)))
    return jnp.moveaxis(h, 0, 2).reshape(B, H, L, v.shape[-1])


def mlstm_bidir(q, k, v, g_fwd, g_bwd, st_f, st_b):
    h_f = mlstm_chunkwise(q, k, v, g_fwd[0], g_fwd[1], st_f)
    h_b = mlstm_chunkwise(_flip(q), _flip(k), _flip(v), _flip(g_bwd[0]), _flip(g_bwd[1]), st_b)
    return h_f + _flip(h_b)


def mlstm_out(h, o, g):
    B, H, L, dv = h.shape
    h = h * lax.rsqrt(jnp.mean(h * h, axis=-1, keepdims=True) + EPS)
    h = h.transpose(0, 2, 1, 3).reshape(B, L, H * dv).astype(o.dtype)
    return h * g * jax.nn.sigmoid(o)


FFT_R = 128
FFT_N = FFT_R * FFT_R
HY_FEAT_PAD = 32
HIGHEST = lax.Precision.HIGHEST


def _dft_tables():
    r = np.arange(FFT_R)
    ang = 2.0 * np.pi * np.outer(r, r) / FFT_R
    c, s = np.cos(ang), np.sin(ang)
    first = np.concatenate([c, -s], axis=0)
    mid_fwd = np.block([[c, s], [-s, c]])
    mid_inv = np.block([[c, -s], [s, c]])
    last = np.concatenate([c, -s], axis=1) / FFT_N
    tw = 2.0 * np.pi * np.outer(r, r) / FFT_N
    f32 = lambda a: jnp.asarray(a, jnp.float32)
    return f32(first), f32(mid_fwd), f32(mid_inv), f32(last), f32(np.cos(tw)), f32(np.sin(tw))


def _fft_mid_kernel(a_ref, tc_ref, ts_ref, mf_ref, mi_ref, *rest, inverse):
    if inverse:
        g_ref, o_ref, b_ref = rest
    else:
        o_ref, b_ref = rest
    cb = a_ref.shape[-1]
    tc = tc_ref[0]
    ts = ts_ref[0]
    for lt in range(cb // LANES):
        ch = slice(lt * LANES, (lt + 1) * LANES)
        ar = a_ref[0, 0, :, ch]
        ai = a_ref[1, 0, :, ch]
        b_ref[0:FFT_R, ch] = (ar * tc + ai * ts).astype(b_ref.dtype)
        b_ref[FFT_R:, ch] = (ai * tc - ar * ts).astype(b_ref.dtype)
    x = jnp.dot(mf_ref[...], b_ref[...], preferred_element_type=jnp.float32)
    if not inverse:
        o_ref[0, 0] = x[:FFT_R].astype(o_ref.dtype)
        o_ref[1, 0] = x[FFT_R:].astype(o_ref.dtype)
        return
    xr, xi = x[:FFT_R], x[FFT_R:]
    gr = g_ref[0, 0].astype(jnp.float32)
    gi = g_ref[1, 0].astype(jnp.float32)
    b_ref[0:FFT_R, :] = (xr * gr - xi * gi).astype(b_ref.dtype)
    b_ref[FFT_R:, :] = (xr * gi + xi * gr).astype(b_ref.dtype)
    q = jnp.dot(mi_ref[...], b_ref[...], preferred_element_type=jnp.float32)
    for lt in range(cb // LANES):
        ch = slice(lt * LANES, (lt + 1) * LANES)
        qr = q[:FFT_R, ch]
        qi = q[FFT_R:, ch]
        o_ref[0, 0, :, ch] = (qr * tc - qi * ts).astype(o_ref.dtype)
        o_ref[1, 0, :, ch] = (qi * tc + qr * ts).astype(o_ref.dtype)


def fft_mid(a, tc, ts, mid_fwd, mid_inv, g=None, cb=1024, out_dtype=jnp.float32):
    C = a.shape[-1]
    cb = min(cb, C)
    inverse = g is not None
    blk = pl.BlockSpec((2, 1, FFT_R, cb), lambda k, j: (0, k, 0, j))
    tw = pl.BlockSpec((1, FFT_R, LANES), lambda k, j: (k, 0, 0))
    mat = pl.BlockSpec((2 * FFT_R, 2 * FFT_R), lambda k, j: (0, 0))
    return pl.pallas_call(
        functools.partial(_fft_mid_kernel, inverse=inverse),
        grid=(FFT_R, C // cb),
        in_specs=[blk, tw, tw, mat, mat] + ([blk] if inverse else []),
        out_specs=blk,
        out_shape=jax.ShapeDtypeStruct(a.shape, out_dtype),
        scratch_shapes=[pltpu.VMEM((2 * FFT_R, cb), jnp.bfloat16)],
        compiler_params=pltpu.CompilerParams(
            dimension_semantics=("parallel", "parallel"),
            vmem_limit_bytes=VMEM_LIMIT_BYTES),
    )(a, tc, ts, mid_fwd.astype(jnp.bfloat16), mid_inv.astype(jnp.bfloat16), *([g] if inverse else []))


def _fft_last_kernel(f_ref, q_ref, z_ref, x_ref, b_ref, o_ref):
    y = jnp.dot(f_ref[...], q_ref[...].astype(jnp.bfloat16), preferred_element_type=jnp.float32)
    o_ref[...] = (x_ref[...] * (y + z_ref[...] * b_ref[...])).astype(o_ref.dtype)


def fft_last(last, q, z, xg, bias_row, tn=2048, out_dtype=jnp.float32):
    rows, cols = z.shape
    return pl.pallas_call(
        _fft_last_kernel,
        grid=(cols // tn,),
        in_specs=[pl.BlockSpec((rows, 2 * FFT_R), lambda j: (0, 0)),
                  pl.BlockSpec((2 * FFT_R, tn), lambda j: (0, j)),
                  pl.BlockSpec((rows, tn), lambda j: (0, j)),
                  pl.BlockSpec((rows, tn), lambda j: (0, j)),
                  pl.BlockSpec((1, tn), lambda j: (0, j))],
        out_specs=pl.BlockSpec((rows, tn), lambda j: (0, j)),
        out_shape=jax.ShapeDtypeStruct((rows, cols), out_dtype),
        compiler_params=pltpu.CompilerParams(
            dimension_semantics=("parallel",), vmem_limit_bytes=VMEM_LIMIT_BYTES),
    )(last[:rows].astype(jnp.bfloat16), q, z, xg, bias_row)


def _hyena_filter_kernel(ft_ref, tn_ref, w1_ref, b1_ref, w2_ref, b2_ref, fr_ref, w3_ref, dl_ref, o_ref):
    pre = jnp.dot(ft_ref[...], w1_ref[...], preferred_element_type=jnp.float32, precision=HIGHEST)
    hdn = jnp.sin(fr_ref[...] * (pre + b1_ref[...]))
    pre = jnp.dot(hdn, w2_ref[...], preferred_element_type=jnp.float32, precision=HIGHEST)
    hdn = jnp.sin(fr_ref[...] * (pre + b2_ref[...]))
    filt = jnp.dot(hdn, w3_ref[0], preferred_element_type=jnp.float32, precision=HIGHEST)
    tn = tn_ref[...]
    cols = o_ref.shape[-1]
    for lt in range(cols // LANES):
        ch = slice(lt * LANES, (lt + 1) * LANES)
        window = jnp.exp(-tn * dl_ref[:, ch]) + HY_SHIFT
        o_ref[0, :, ch] = filt[:, ch] * window


def hyena_filter_taps(L, w1, b1, w2, b2, w3, freq, tr=512):
    assert FFT_N == 2 * L
    n = np.arange(FFT_N)
    t = np.where(n < L, n, (FFT_N - n) % L).astype(np.float32)
    tnorm = t / np.float32(L)
    bands = np.linspace(1e-4, HY_BANDS - 1, HY_BANDS, dtype=np.float32)
    ang = (np.float32(2.0 * math.pi / L) * t[:, None] * bands[None, :]).astype(np.float64)
    feats = np.concatenate([tnorm[:, None], np.cos(ang), -np.sin(ang)], axis=-1)
    feats = np.pad(feats, ((0, 0), (0, HY_FEAT_PAD - feats.shape[1]))).astype(np.float32)
    tmark = np.broadcast_to(tnorm[:, None], (FFT_N, LANES))
    deltas = np.abs(np.linspace(HY_MIN_DECAY, HY_MAX_DECAY, HY_W, dtype=np.float32))[None, :]
    w1p = jnp.pad(w1, ((0, HY_FEAT_PAD - w1.shape[0]), (0, 0)))
    ffn = w2.shape[0]
    w3r = w3.reshape(ffn, 2 * HY_ORDER, HY_W).transpose(1, 0, 2)
    half = L // tr
    row = lambda o, r: (r, 0)
    fix = lambda o, r: (0, 0)
    return pl.pallas_call(
        _hyena_filter_kernel,
        grid=(HY_ORDER, FFT_N // tr),
        in_specs=[pl.BlockSpec((tr, HY_FEAT_PAD), row),
                  pl.BlockSpec((tr, LANES), row),
                  pl.BlockSpec((HY_FEAT_PAD, ffn), fix),
                  pl.BlockSpec((1, ffn), fix),
                  pl.BlockSpec((ffn, ffn), fix),
                  pl.BlockSpec((1, ffn), fix),
                  pl.BlockSpec((1, ffn), fix),
                  pl.BlockSpec((1, ffn, HY_W), lambda o, r: ((r // half) * HY_ORDER + o, 0, 0)),
                  pl.BlockSpec((1, HY_W), fix)],
        out_specs=pl.BlockSpec((1, tr, HY_W), lambda o, r: (o, r, 0)),
        out_shape=jax.ShapeDtypeStruct((HY_ORDER, FFT_N, HY_W), jnp.float32),
        compiler_params=pltpu.CompilerParams(
            dimension_semantics=("parallel", "parallel"), vmem_limit_bytes=VMEM_LIMIT_BYTES),
    )(jnp.asarray(feats), jnp.asarray(tmark), w1p, b1[None], w2, b2[None], freq[None], w3r, jnp.asarray(deltas))


def hyena(xs, taps, bias):
    L = xs.shape[0]
    C = xs.shape[1] // 3
    first, mid_fwd, mid_inv, last, tcos, tsin = _dft_tables()
    tc = jnp.broadcast_to(tcos[:, :, None], (FFT_R, FFT_R, LANES))
    ts = jnp.broadcast_to(tsin[:, :, None], (FFT_R, FFT_R, LANES))
    rows = L // FFT_R
    z = xs[:, 2 * C:]
    for o in range(HY_ORDER):
        ga = pmm(first, taps[o].reshape(FFT_R, FFT_R * C), tm=2 * FFT_R, tn=2048)
        g = fft_mid(ga.reshape(2, FFT_R, FFT_R, C), tc, ts, mid_fwd, mid_inv)
        a = pmm(first[:, :rows], z.reshape(rows, FFT_R * C), tm=2 * FFT_R, tn=2048)
        q = fft_mid(a.reshape(2, FFT_R, FFT_R, C), tc, ts, mid_fwd, mid_inv, g=g)
        xg = xs[:, o * C:(o + 1) * C].reshape(rows, FFT_R * C)
        bias_row = jnp.tile(bias[o] + taps[o, L], FFT_R)[None, :]
        z = fft_last(last, q.reshape(2 * FFT_R, FFT_R * C), z.reshape(rows, FFT_R * C), xg, bias_row).reshape(L, C)
    return z


def merge(h_ml, h_hy, bg, w_pm, w_ph, w_o):
    g_ml, g_hy = jnp.split(bg, 2, axis=-1)
    y = jax.nn.sigmoid(g_ml) * pmm(h_ml[0], w_pm)[None] + jax.nn.sigmoid(g_hy) * pmm(h_hy[0], w_ph)[None]
    return pmm(y[0], w_o)[None]


LANES = 128
NEG_INF = float('-inf')
_CAND_ROWS = tuple(PEER_TOPK // (i + 1) for i in range(PEER_TOPK))


def _extract_top(s, rounds):
    rows = s.shape[0]
    iota = lax.broadcasted_iota(jnp.int32, s.shape, 0)
    tops = []
    for _ in range(rounds):
        m = jnp.max(s, axis=0, keepdims=True)
        first = jnp.min(jnp.where(s == m, iota, rows), axis=0, keepdims=True)
        s = jnp.where(iota == first, NEG_INF, s)
        tops.append(m)
    return tops


def _peer_topk_kernel(q_ref, k_ref, s1_ref, s2_ref, e1_ref, e2_ref, thr_ref):
    tk = q_ref.shape[0]
    for u in range(tk // LANES):
        tok = slice(u * LANES, (u + 1) * LANES)
        s = []
        tops = []
        for p in range(2):
            qp = q_ref[tok, p * PEER_NKEYS:(p + 1) * PEER_NKEYS]
            sp = lax.dot_general(k_ref[0, p], qp, (((1,), (1,)), ((), ())),
                                 preferred_element_type=jnp.float32)
            s.append(sp)
            tops.append(_extract_top(sp, PEER_TOPK))
        top2a = jnp.concatenate(tops[1][:8], axis=0)
        top2b = jnp.concatenate(tops[1][8:], axis=0)
        row = lax.broadcasted_iota(jnp.int32, (8, LANES), 0)
        cands = [tops[0][0] + top2a, tops[0][0] + top2b]
        for i in range(1, PEER_TOPK):
            cands.append(jnp.where(row < _CAND_ROWS[i], tops[0][i] + top2a, NEG_INF))
        best = _extract_top(jnp.concatenate(cands, axis=0), PEER_TOPK)
        z = jnp.ones_like(best[0])
        for b in best[1:]:
            z = z + jnp.exp(b - best[0])
        s1_ref[0, :, tok] = s[0]
        s2_ref[0, :, tok] = s[1]
        e1_ref[0, :, tok] = jnp.exp(s[0] - tops[0][0]) / z
        e2_ref[0, :, tok] = jnp.exp(s[1] - tops[1][0])
        thr_ref[0, :, tok] = best[-1]


def peer_topk(q, sub_keys, tk=256):
    T = q.shape[0]
    H = PEER_HEADS
    big = jax.ShapeDtypeStruct((H, PEER_NKEYS, T), jnp.float32)
    big_spec = pl.BlockSpec((1, PEER_NKEYS, tk), lambda i, h: (h, 0, i))
    s1, s2, e1, e2, thr = pl.pallas_call(
        _peer_topk_kernel,
        grid=(T // tk, H),
        in_specs=[pl.BlockSpec((tk, PEER_DKEY), lambda i, h: (i, h)),
                  pl.BlockSpec((1, 2, PEER_NKEYS, PEER_DKEY // 2), lambda i, h: (h, 0, 0, 0))],
        out_specs=[big_spec, big_spec, big_spec, big_spec,
                   pl.BlockSpec((1, 1, tk), lambda i, h: (h, 0, i))],
        out_shape=[big, big, big, big, jax.ShapeDtypeStruct((H, 1, T), jnp.float32)],
        compiler_params=pltpu.CompilerParams(
            dimension_semantics=("parallel", "parallel"),
            vmem_limit_bytes=VMEM_LIMIT_BYTES),
    )(q, sub_keys.astype(jnp.bfloat16))
    return s1, s2, e1, e2, thr.reshape(H, T)


def _peer_expert_kernel(v_ref, u_ref, vt_ref, s1a_ref, e1a_ref, s2_ref, e2_ref, thr_ref, o_ref, sc_ref, w_ref):
    j = pl.program_id(1)
    tt = v_ref.shape[0]
    na = u_ref.shape[0] // PEER_NKEYS

    @pl.when(j == 0)
    def _():
        o_ref[...] = jnp.zeros_like(o_ref)

    sc_ref[...] = lax.dot_general(u_ref[...], v_ref[...], (((1,), (1,)), ((), ())),
                                  preferred_element_type=jnp.float32)

    for u in range(tt // LANES):
        tok = slice(u * LANES, (u + 1) * LANES)

        def per_a(al, carry):
            g = jnp.zeros((PEER_NKEYS, LANES), jnp.float32)
            for h in range(PEER_HEADS):
                s1row = s1a_ref[al, h:h + 1, tok]
                e1row = e1a_ref[al, h:h + 1, tok]
                keep = (s1row + s2_ref[h, :, tok]) >= thr_ref[h:h + 1, tok]
                g = g + jnp.where(keep, e1row * e2_ref[h, :, tok], 0.0)
            rows = pl.ds(pl.multiple_of(al * PEER_NKEYS, PEER_NKEYS), PEER_NKEYS)
            sc = sc_ref[rows, tok]
            act = 0.5 * sc * (1.0 + lax.erf(sc * math.sqrt(0.5)))
            w_ref[rows, tok] = (act * g).astype(w_ref.dtype)
            return carry

        lax.fori_loop(0, na, per_a, 0)

    o_ref[...] += jnp.dot(vt_ref[...], w_ref[...], preferred_element_type=jnp.float32)


def peer_experts(v, expert_u, expert_vt, s1a, e1a, s2, e2, thr, tt=512, nb=1024):
    T, D = v.shape
    N = expert_u.shape[0]
    na = nb // PEER_NKEYS
    H = PEER_HEADS
    return pl.pallas_call(
        _peer_expert_kernel,
        grid=(T // tt, N // nb),
        in_specs=[pl.BlockSpec((tt, D), lambda i, j: (i, 0)),
                  pl.BlockSpec((nb, D), lambda i, j: (j, 0)),
                  pl.BlockSpec((D, nb), lambda i, j: (0, j)),
                  pl.BlockSpec((na, H, tt), lambda i, j: (j, 0, i)),
                  pl.BlockSpec((na, H, tt), lambda i, j: (j, 0, i)),
                  pl.BlockSpec((H, PEER_NKEYS, tt), lambda i, j: (0, 0, i)),
                  pl.BlockSpec((H, PEER_NKEYS, tt), lambda i, j: (0, 0, i)),
                  pl.BlockSpec((H, tt), lambda i, j: (0, i))],
        out_specs=pl.BlockSpec((D, tt), lambda i, j: (0, i)),
        out_shape=jax.ShapeDtypeStruct((D, T), jnp.float32),
        scratch_shapes=[pltpu.VMEM((nb, tt), jnp.float32), pltpu.VMEM((nb, tt), jnp.bfloat16)],
        compiler_params=pltpu.CompilerParams(
            dimension_semantics=("parallel", "arbitrary"),
            vmem_limit_bytes=VMEM_LIMIT_BYTES),
    )(v, expert_u, expert_vt, s1a, e1a, s2, e2, thr)


def peer(u, w_q, sub_keys, expert_u, expert_v, tk=256, tt=512, nb=1024):
    ub = u.astype(jnp.bfloat16)
    q = pmm(ub, w_q.astype(jnp.bfloat16), out_dtype=jnp.bfloat16)
    s1, s2, e1, e2, thr = peer_topk(q, sub_keys, tk=tk)
    out_t = peer_experts(ub, expert_u.astype(jnp.bfloat16), expert_v.astype(jnp.bfloat16).T,
                         s1.transpose(1, 0, 2), e1.transpose(1, 0, 2), s2, e2, thr, tt=tt, nb=nb)
    return out_t.T


def kernel(x, c, ctx, c_ctx, w_mod, b_mod, norm1_g, norm2_g, final_g, w_in, ml_conv_w, ml_gate_b, ml_norm_g, hy_conv_w, hy_w1, hy_b1, hy_w2, hy_b2, hy_w3, hy_freq, hy_bias, w_proj_ml, w_proj_hy, w_out, peer_wq, peer_keys, peer_u, peer_v):
    B, L, D = x.shape
    rows = L // GRID_W
    l = 0
    s_lat = jax.nn.silu(c)[:, None, :]
    s_ctx = jax.nn.silu(c_ctx)
    h_lat, h_ctx = x, ctx
    sh1, sc1, g1, sh2, sc2, g2 = jnp.split(s_lat @ w_mod[l] + b_mod[l], 6, axis=-1)
    csh1, csc1, cg1, csh2, csc2, cg2 = jnp.split(s_ctx @ w_mod[l] + b_mod[l], 6, axis=-1)

    u_lat = rmsnorm(h_lat, norm1_g[l]) * (1.0 + sc1) + sh1
    u_ctx = rmsnorm(h_ctx, norm1_g[l]) * (1.0 + csc1) + csh1
    w_in_p = jnp.pad(w_in[l], ((0, 0), (0, 16896 - w_in.shape[-1])))
    p_lat = pmm(u_lat[0], w_in_p)[None, :, :w_in.shape[-1]]
    p_ctx = pmm(u_ctx[0], w_in_p)[None, :, :w_in.shape[-1]]
    qk_l, v_l, o_l, gt_l, hy_l, bg_l = jnp.split(p_lat, IN_SPLIT, axis=-1)
    qk_c, v_c, o_c, gt_c, hy_c, bg_c = jnp.split(p_ctx, IN_SPLIT, axis=-1)

    qk_l = jax.nn.silu(dwconv_grid(qk_l, ml_conv_w[l], rows))
    qk_c = jax.nn.silu(dwconv1d(qk_c, ml_conv_w[l][1]))
    q_l, k_l, vh_l, gf_l, gb_l = mlstm_heads(qk_l, v_l, gt_l, ml_gate_b[l])
    q_c, k_c, vh_c, gf_c, gb_c = mlstm_heads(qk_c, v_c, gt_c, ml_gate_b[l])
    st_f = mlstm_final_state(k_c, vh_c, gf_c[0], gf_c[1])
    st_b = mlstm_final_state(_flip(k_c), _flip(vh_c), _flip(gb_c[0]), _flip(gb_c[1]))
    h_ml_l = mlstm_out(mlstm_bidir(q_l, k_l, vh_l, gf_l, gb_l, st_f, st_b), o_l, ml_norm_g[l])

    taps = hyena_filter_taps(L, hy_w1[l], hy_b1[l], hy_w2[l], hy_b2[l], hy_w3[l], hy_freq[l])
    h_hy_l = hyena(dwconv1d(hy_l, hy_conv_w[l])[0], taps, hy_bias[l])[None]
    mix_l = merge(h_ml_l, h_hy_l, bg_l, w_proj_ml[l], w_proj_hy[l], w_out[l])

    h_lat = h_lat + g1 * mix_l
    v_lat = rmsnorm(h_lat, norm2_g[l]) * (1.0 + sc2) + sh2
    h_lat = h_lat + g2 * peer(v_lat[0], peer_wq[l], peer_keys[l], peer_u[l], peer_v[l])[None]
    return rmsnorm(h_lat, final_g)
```

```python
import functools
import math
import numpy as np
import jax
import jax.numpy as jnp
from jax import lax
from jax.experimental import pallas as pl
from jax.experimental.pallas import tpu as pltpu

D_MODEL = 2048
GRID_W = 64
EPS = 1e-6

ML_HEADS = 8
ML_DQK = D_MODEL // 16
ML_DV = D_MODEL // 8
ML_QK_W = ML_HEADS * ML_DQK
ML_V_W = ML_HEADS * ML_DV
ML_CHUNK = 64

HY_W = D_MODEL
HY_ORDER = 2
HY_BANDS = 8
HY_SHIFT = 0.05
HY_MIN_DECAY = math.log(1e-2) / 1.5
HY_MAX_DECAY = math.log(1e-2) / 0.3

PEER_HEADS = 8
PEER_NKEYS = 128
PEER_DKEY = 256
PEER_TOPK = 16
PEER_BLOCK = 128

IN_SPLIT = (2 * ML_QK_W,
            2 * ML_QK_W + ML_V_W,
            2 * ML_QK_W + 2 * ML_V_W,
            2 * ML_QK_W + 2 * ML_V_W + 4 * ML_HEADS,
            2 * ML_QK_W + 2 * ML_V_W + 4 * ML_HEADS + 3 * HY_W)

VMEM_LIMIT_BYTES = 56 * 1024 * 1024


def _mm_kernel(a_ref, b_ref, o_ref):
    o_ref[...] = jnp.dot(a_ref[...].astype(jnp.bfloat16), b_ref[...].astype(jnp.bfloat16),
                         preferred_element_type=jnp.float32).astype(o_ref.dtype)


def pmm(a, b, tm=512, tn=512, out_dtype=jnp.float32):
    M, K = a.shape
    _, N = b.shape
    tm = min(tm, M)
    tn = min(tn, N)
    assert M % tm == 0 and N % tn == 0, (M, N, tm, tn)
    return pl.pallas_call(
        _mm_kernel,
        grid=(N // tn, M // tm),
        in_specs=[pl.BlockSpec((tm, K), lambda j, i: (i, 0)),
                  pl.BlockSpec((K, tn), lambda j, i: (0, j))],
        out_specs=pl.BlockSpec((tm, tn), lambda j, i: (i, j)),
        out_shape=jax.ShapeDtypeStruct((M, N), out_dtype),
        compiler_params=pltpu.CompilerParams(
            dimension_semantics=("parallel", "parallel"),
            vmem_limit_bytes=VMEM_LIMIT_BYTES),
    )(a, b)


def rmsnorm(x, g):
    xf = x.astype(jnp.float32)
    y = xf * lax.rsqrt(jnp.mean(xf * xf, axis=-1, keepdims=True) + EPS)
    return y.astype(x.dtype) * g


def dwconv1d(x, w):
    K = w.shape[0]
    L = x.shape[1]
    pad = K // 2
    xp = jnp.pad(x, ((0, 0), (pad, pad), (0, 0)))
    y = xp[:, 0:L] * w[0]
    for j in range(1, K):
        y = y + xp[:, j:j + L] * w[j]
    return y


def dwconv_grid(x, w, rows):
    B, L, C = x.shape
    img = x.reshape(B, rows, GRID_W, C)
    y = lax.conv_general_dilated(img, w[:, :, None, :], (1, 1), 'SAME',
                                 dimension_numbers=('NHWC', 'HWIO', 'NHWC'),
                                 feature_group_count=C)
    return y.reshape(B, L, C)


def _flip(a):
    return jnp.flip(a, axis=2)


def mlstm_heads(qk, v, gates, gate_b):
    B, L, _ = v.shape
    q, k = jnp.split(qk, 2, axis=-1)
    q = q.reshape(B, L, ML_HEADS, ML_DQK).transpose(0, 2, 1, 3).astype(jnp.float32) * (ML_DQK ** -0.5)
    k = k.reshape(B, L, ML_HEADS, ML_DQK).transpose(0, 2, 1, 3).astype(jnp.float32)
    vh = v.reshape(B, L, ML_HEADS, ML_DV).transpose(0, 2, 1, 3).astype(jnp.float32)
    g = gates.astype(jnp.float32).reshape(B, L, 4, ML_HEADS).transpose(2, 0, 3, 1) \
        + gate_b.astype(jnp.float32)[:, None, :, None]
    g_fwd = (g[0], jax.nn.log_sigmoid(g[1]))
    g_bwd = (g[2], jax.nn.log_sigmoid(g[3]))
    return q, k, vh, g_fwd, g_bwd


def mlstm_final_state(k, v, log_i, log_f):
    b = jnp.cumsum(log_f, axis=-1)
    dec = b[..., -1:] - b + log_i
    m = jnp.max(dec, axis=-1)
    w = jnp.exp(dec - m[..., None])
    C = jnp.einsum('bhs,bhsd,bhsv->bhdv', w, k, v)
    n = jnp.einsum('bhs,bhsd->bhd', w, k)
    return (C, n, m)


def mlstm_chunkwise(q, k, v, log_i, log_f, state):
    B, H, L, _ = q.shape
    T = ML_CHUNK
    nc = L // T

    def to_chunks(a):
        return jnp.moveaxis(a.reshape(a.shape[:2] + (nc, T) + a.shape[3:]), 2, 0)

    causal = jnp.tril(jnp.ones((T, T), dtype=bool))

    def step(carry, inp):
        C, n, m = carry
        qb, kb, vb, ib, fb = inp
        b = jnp.cumsum(fb, axis=-1)
        D = b[..., :, None] - b[..., None, :] + ib[..., None, :]
        D = jnp.where(causal, D, -jnp.inf)
        inter = b + m[..., None]
        m_t = jnp.maximum(inter, jnp.max(D, axis=-1))
        w = jnp.exp(D - m_t[..., None])
        s_inter = jnp.exp(inter - m_t)
        qk = jnp.einsum('bhtd,bhsd->bhts', qb, kb) * w
        num = jnp.einsum('bhts,bhsv->bhtv', qk, vb) \
            + s_inter[..., None] * jnp.einsum('bhtd,bhdv->bhtv', qb, C)
        den = jnp.sum(qk, axis=-1) + s_inter * jnp.einsum('bhtd,bhd->bht', qb, n)
        h = num / jnp.maximum(jnp.abs(den), jnp.exp(-m_t))[..., None]
        bT = b[..., -1]
        dec = bT[..., None] - b + ib
        m_new = jnp.maximum(bT + m, jnp.max(dec, axis=-1))
        wk = jnp.exp(dec - m_new[..., None])
        s_old = jnp.exp(bT + m - m_new)
        C_new = s_old[..., None, None] * C + jnp.einsum('bhs,bhsd,bhsv->bhdv', wk, kb, vb)
        n_new = s_old[..., None] * n + jnp.einsum('bhs,bhsd->bhd', wk, kb)
        return (C_new, n_new, m_new), h

    _, h = lax.scan(step, state, (to_chunks(q), to_chunks(k), to_chunks(v),
                                  to_chunks(log_i), to_chunks(log_f)))
    return jnp.moveaxis(h, 0, 2).reshape(B, H, L, v.shape[-1])


def mlstm_bidir(q, k, v, g_fwd, g_bwd, st_f, st_b):
    h_f = mlstm_chunkwise(q, k, v, g_fwd[0], g_fwd[1], st_f)
    h_b = mlstm_chunkwise(_flip(q), _flip(k), _flip(v), _flip(g_bwd[0]), _flip(g_bwd[1]), st_b)
    return h_f + _flip(h_b)


def mlstm_out(h, o, g):
    B, H, L, dv = h.shape
    h = h * lax.rsqrt(jnp.mean(h * h, axis=-1, keepdims=True) + EPS)
    h = h.transpose(0, 2, 1, 3).reshape(B, L, H * dv).astype(o.dtype)
    return h * g * jax.nn.sigmoid(o)


FFT_R = 128
FFT_N = FFT_R * FFT_R
HY_FEAT_PAD = 32
HIGHEST = lax.Precision.HIGHEST


def _dft_tables():
    r = np.arange(FFT_R)
    ang = 2.0 * np.pi * np.outer(r, r) / FFT_R
    c, s = np.cos(ang), np.sin(ang)
    first = np.concatenate([c, -s], axis=0)
    mid_fwd = np.block([[c, s], [-s, c]])
    mid_inv = np.block([[c, -s], [s, c]])
    last = np.concatenate([c, -s], axis=1) / FFT_N
    tw = 2.0 * np.pi * np.outer(r, r) / FFT_N
    f32 = lambda a: jnp.asarray(a, jnp.float32)
    return f32(first), f32(mid_fwd), f32(mid_inv), f32(last), f32(np.cos(tw)), f32(np.sin(tw))


def _fft_mid_kernel(a_ref, tc_ref, ts_ref, mf_ref, mi_ref, *rest, inverse):
    if inverse:
        g_ref, o_ref, b_ref = rest
    else:
        o_ref, b_ref = rest
    cb = a_ref.shape[-1]
    tc = tc_ref[0]
    ts = ts_ref[0]
    for lt in range(cb // LANES):
        ch = slice(lt * LANES, (lt + 1) * LANES)
        ar = a_ref[0, 0, :, ch]
        ai = a_ref[1, 0, :, ch]
        b_ref[0:FFT_R, ch] = (ar * tc + ai * ts).astype(b_ref.dtype)
        b_ref[FFT_R:, ch] = (ai * tc - ar * ts).astype(b_ref.dtype)
    x = jnp.dot(mf_ref[...], b_ref[...], preferred_element_type=jnp.float32)
    if not inverse:
        o_ref[0, 0] = x[:FFT_R].astype(o_ref.dtype)
        o_ref[1, 0] = x[FFT_R:].astype(o_ref.dtype)
        return
    xr, xi = x[:FFT_R], x[FFT_R:]
    gr = g_ref[0, 0].astype(jnp.float32)
    gi = g_ref[1, 0].astype(jnp.float32)
    b_ref[0:FFT_R, :] = (xr * gr - xi * gi).astype(b_ref.dtype)
    b_ref[FFT_R:, :] = (xr * gi + xi * gr).astype(b_ref.dtype)
    q = jnp.dot(mi_ref[...], b_ref[...], preferred_element_type=jnp.float32)
    for lt in range(cb // LANES):
        ch = slice(lt * LANES, (lt + 1) * LANES)
        qr = q[:FFT_R, ch]
        qi = q[FFT_R:, ch]
        o_ref[0, 0, :, ch] = (qr * tc - qi * ts).astype(o_ref.dtype)
        o_ref[1, 0, :, ch] = (qi * tc + qr * ts).astype(o_ref.dtype)


def fft_mid(a, tc, ts, mid_fwd, mid_inv, g=None, cb=1024, out_dtype=jnp.float32):
    C = a.shape[-1]
    cb = min(cb, C)
    inverse = g is not None
    blk = pl.BlockSpec((2, 1, FFT_R, cb), lambda k, j: (0, k, 0, j))
    tw = pl.BlockSpec((1, FFT_R, LANES), lambda k, j: (k, 0, 0))
    mat = pl.BlockSpec((2 * FFT_R, 2 * FFT_R), lambda k, j: (0, 0))
    return pl.pallas_call(
        functools.partial(_fft_mid_kernel, inverse=inverse),
        grid=(FFT_R, C // cb),
        in_specs=[blk, tw, tw, mat, mat] + ([blk] if inverse else []),
        out_specs=blk,
        out_shape=jax.ShapeDtypeStruct(a.shape, out_dtype),
        scratch_shapes=[pltpu.VMEM((2 * FFT_R, cb), jnp.bfloat16)],
        compiler_params=pltpu.CompilerParams(
            dimension_semantics=("parallel", "parallel"),
            vmem_limit_bytes=VMEM_LIMIT_BYTES),
    )(a, tc, ts, mid_fwd.astype(jnp.bfloat16), mid_inv.astype(jnp.bfloat16), *([g] if inverse else []))


def _fft_last_kernel(f_ref, q_ref, z_ref, x_ref, b_ref, o_ref):
    y = jnp.dot(f_ref[...], q_ref[...].astype(jnp.bfloat16), preferred_element_type=jnp.float32)
    o_ref[...] = (x_ref[...] * (y + z_ref[...] * b_ref[...])).astype(o_ref.dtype)


def fft_last(last, q, z, xg, bias_row, tn=2048, out_dtype=jnp.float32):
    rows, cols = z.shape
    return pl.pallas_call(
        _fft_last_kernel,
        grid=(cols // tn,),
        in_specs=[pl.BlockSpec((rows, 2 * FFT_R), lambda j: (0, 0)),
                  pl.BlockSpec((2 * FFT_R, tn), lambda j: (0, j)),
                  pl.BlockSpec((rows, tn), lambda j: (0, j)),
                  pl.BlockSpec((rows, tn), lambda j: (0, j)),
                  pl.BlockSpec((1, tn), lambda j: (0, j))],
        out_specs=pl.BlockSpec((rows, tn), lambda j: (0, j)),
        out_shape=jax.ShapeDtypeStruct((rows, cols), out_dtype),
        compiler_params=pltpu.CompilerParams(
            dimension_semantics=("parallel",), vmem_limit_bytes=VMEM_LIMIT_BYTES),
    )(last[:rows].astype(jnp.bfloat16), q, z, xg, bias_row)


def _hyena_filter_kernel(ft_ref, tn_ref, w1_ref, b1_ref, w2_ref, b2_ref, fr_ref, w3_ref, dl_ref, o_ref):
    pre = jnp.dot(ft_ref[...], w1_ref[...], preferred_element_type=jnp.float32, precision=HIGHEST)
    hdn = jnp.sin(fr_ref[...] * (pre + b1_ref[...]))
    pre = jnp.dot(hdn, w2_ref[...], preferred_element_type=jnp.float32, precision=HIGHEST)
    hdn = jnp.sin(fr_ref[...] * (pre + b2_ref[...]))
    filt = jnp.dot(hdn, w3_ref[0], preferred_element_type=jnp.float32, precision=HIGHEST)
    tn = tn_ref[...]
    cols = o_ref.shape[-1]
    for lt in range(cols // LANES):
        ch = slice(lt * LANES, (lt + 1) * LANES)
        window = jnp.exp(-tn * dl_ref[:, ch]) + HY_SHIFT
        o_ref[0, :, ch] = filt[:, ch] * window


def hyena_filter_taps(L, w1, b1, w2, b2, w3, freq, tr=512):
    assert FFT_N == 2 * L
    n = np.arange(FFT_N)
    t = np.where(n < L, n, (FFT_N - n) % L).astype(np.float32)
    tnorm = t / np.float32(L)
    bands = np.linspace(1e-4, HY_BANDS - 1, HY_BANDS, dtype=np.float32)
    ang = (np.float32(2.0 * math.pi / L) * t[:, None] * bands[None, :]).astype(np.float64)
    feats = np.concatenate([tnorm[:, None], np.cos(ang), -np.sin(ang)], axis=-1)
    feats = np.pad(feats, ((0, 0), (0, HY_FEAT_PAD - feats.shape[1]))).astype(np.float32)
    tmark = np.broadcast_to(tnorm[:, None], (FFT_N, LANES))
    deltas = np.abs(np.linspace(HY_MIN_DECAY, HY_MAX_DECAY, HY_W, dtype=np.float32))[None, :]
    w1p = jnp.pad(w1, ((0, HY_FEAT_PAD - w1.shape[0]), (0, 0)))
    ffn = w2.shape[0]
    w3r = w3.reshape(ffn, 2 * HY_ORDER, HY_W).transpose(1, 0, 2)
    half = L // tr
    row = lambda o, r: (r, 0)
    fix = lambda o, r: (0, 0)
    return pl.pallas_call(
        _hyena_filter_kernel,
        grid=(HY_ORDER, FFT_N // tr),
        in_specs=[pl.BlockSpec((tr, HY_FEAT_PAD), row),
                  pl.BlockSpec((tr, LANES), row),
                  pl.BlockSpec((HY_FEAT_PAD, ffn), fix),
                  pl.BlockSpec((1, ffn), fix),
                  pl.BlockSpec((ffn, ffn), fix),
                  pl.BlockSpec((1, ffn), fix),
                  pl.BlockSpec((1, ffn), fix),
                  pl.BlockSpec((1, ffn, HY_W), lambda o, r: ((r // half) * HY_ORDER + o, 0, 0)),
                  pl.BlockSpec((1, HY_W), fix)],
        out_specs=pl.BlockSpec((1, tr, HY_W), lambda o, r: (o, r, 0)),
        out_shape=jax.ShapeDtypeStruct((HY_ORDER, FFT_N, HY_W), jnp.float32),
        compiler_params=pltpu.CompilerParams(
            dimension_semantics=("parallel", "parallel"), vmem_limit_bytes=VMEM_LIMIT_BYTES),
    )(jnp.asarray(feats), jnp.asarray(tmark), w1p, b1[None], w2, b2[None], freq[None], w3r, jnp.asarray(deltas))


def hyena(xs, taps, bias):
    L = xs.shape[0]
    C = xs.shape[1] // 3
    first, mid_fwd, mid_inv, last, tcos, tsin = _dft_tables()
    tc = jnp.broadcast_to(tcos[:, :, None], (FFT_R, FFT_R, LANES))
    ts = jnp.broadcast_to(tsin[:, :, None], (FFT_R, FFT_R, LANES))
    rows = L // FFT_R
    z = xs[:, 2 * C:]
    for o in range(HY_ORDER):
        ga = pmm(first, taps[o].reshape(FFT_R, FFT_R * C), tm=2 * FFT_R, tn=2048)
        g = fft_mid(ga.reshape(2, FFT_R, FFT_R, C), tc, ts, mid_fwd, mid_inv)
        a = pmm(first[:, :rows], z.reshape(rows, FFT_R * C), tm=2 * FFT_R, tn=2048)
        q = fft_mid(a.reshape(2, FFT_R, FFT_R, C), tc, ts, mid_fwd, mid_inv, g=g)
        xg = xs[:, o * C:(o + 1) * C].reshape(rows, FFT_R * C)
        bias_row = jnp.tile(bias[o] + taps[o, L], FFT_R)[None, :]
        z = fft_last(last, q.reshape(2 * FFT_R, FFT_R * C), z.reshape(rows, FFT_R * C), xg, bias_row).reshape(L, C)
    return z


def merge(h_ml, h_hy, bg, w_pm, w_ph, w_o):
    g_ml, g_hy = jnp.split(bg, 2, axis=-1)
    y = jax.nn.sigmoid(g_ml) * pmm(h_ml[0], w_pm)[None] + jax.nn.sigmoid(g_hy) * pmm(h_hy[0], w_ph)[None]
    return pmm(y[0], w_o)[None]


LANES = 128
NEG_INF = float('-inf')
_CAND_ROWS = tuple(PEER_TOPK // (i + 1) for i in range(PEER_TOPK))


def _extract_top(s, rounds):
    rows = s.shape[0]
    iota = lax.broadcasted_iota(jnp.int32, s.shape, 0)
    tops = []
    for _ in range(rounds):
        m = jnp.max(s, axis=0, keepdims=True)
        first = jnp.min(jnp.where(s == m, iota, rows), axis=0, keepdims=True)
        s = jnp.where(iota == first, NEG_INF, s)
        tops.append(m)
    return tops


def _peer_topk_kernel(q_ref, k_ref, s1_ref, s2_ref, e1_ref, e2_ref, thr_ref):
    tk = q_ref.shape[0]
    for u in range(tk // LANES):
        tok = slice(u * LANES, (u + 1) * LANES)
        s = []
        tops = []
        for p in range(2):
            qp = q_ref[tok, p * PEER_NKEYS:(p + 1) * PEER_NKEYS]
            sp = lax.dot_general(k_ref[0, p], qp, (((1,), (1,)), ((), ())),
                                 preferred_element_type=jnp.float32)
            s.append(sp)
            tops.append(_extract_top(sp, PEER_TOPK))
        top2a = jnp.concatenate(tops[1][:8], axis=0)
        top2b = jnp.concatenate(tops[1][8:], axis=0)
        row = lax.broadcasted_iota(jnp.int32, (8, LANES), 0)
        cands = [tops[0][0] + top2a, tops[0][0] + top2b]
        for i in range(1, PEER_TOPK):
            cands.append(jnp.where(row < _CAND_ROWS[i], tops[0][i] + top2a, NEG_INF))
        best = _extract_top(jnp.concatenate(cands, axis=0), PEER_TOPK)
        z = jnp.ones_like(best[0])
        for b in best[1:]:
            z = z + jnp.exp(b - best[0])
        s1_ref[0, :, tok] = s[0]
        s2_ref[0, :, tok] = s[1]
        e1_ref[0, :, tok] = jnp.exp(s[0] - tops[0][0]) / z
        e2_ref[0, :, tok] = jnp.exp(s[1] - tops[1][0])
        thr_ref[0, :, tok] = best[-1]


def peer_topk(q, sub_keys, tk=256):
    T = q.shape[0]
    H = PEER_HEADS
    big = jax.ShapeDtypeStruct((H, PEER_NKEYS, T), jnp.float32)
    big_spec = pl.BlockSpec((1, PEER_NKEYS, tk), lambda i, h: (h, 0, i))
    s1, s2, e1, e2, thr = pl.pallas_call(
        _peer_topk_kernel,
        grid=(T // tk, H),
        in_specs=[pl.BlockSpec((tk, PEER_DKEY), lambda i, h: (i, h)),
                  pl.BlockSpec((1, 2, PEER_NKEYS, PEER_DKEY // 2), lambda i, h: (h, 0, 0, 0))],
        out_specs=[big_spec, big_spec, big_spec, big_spec,
                   pl.BlockSpec((1, 1, tk), lambda i, h: (h, 0, i))],
        out_shape=[big, big, big, big, jax.ShapeDtypeStruct((H, 1, T), jnp.float32)],
        compiler_params=pltpu.CompilerParams(
            dimension_semantics=("parallel", "parallel"),
            vmem_limit_bytes=VMEM_LIMIT_BYTES),
    )(q, sub_keys.astype(jnp.bfloat16))
    return s1, s2, e1, e2, thr.reshape(H, T)


def _peer_expert_kernel(v_ref, u_ref, vt_ref, s1a_ref, e1a_ref, s2_ref, e2_ref, thr_ref, o_ref, sc_ref, w_ref):
    j = pl.program_id(1)
    tt = v_ref.shape[0]
    na = u_ref.shape[0] // PEER_NKEYS

    @pl.when(j == 0)
    def _():
        o_ref[...] = jnp.zeros_like(o_ref)

    sc_ref[...] = lax.dot_general(u_ref[...], v_ref[...], (((1,), (1,)), ((), ())),
                                  preferred_element_type=jnp.float32)

    for u in range(tt // LANES):
        tok = slice(u * LANES, (u + 1) * LANES)

        def per_a(al, carry):
            g = jnp.zeros((PEER_NKEYS, LANES), jnp.float32)
            for h in range(PEER_HEADS):
                s1row = s1a_ref[al, h:h + 1, tok]
                e1row = e1a_ref[al, h:h + 1, tok]
                keep = (s1row + s2_ref[h, :, tok]) >= thr_ref[h:h + 1, tok]
                g = g + jnp.where(keep, e1row * e2_ref[h, :, tok], 0.0)
            rows = pl.ds(pl.multiple_of(al * PEER_NKEYS, PEER_NKEYS), PEER_NKEYS)
            sc = sc_ref[rows, tok]
            act = 0.5 * sc * (1.0 + lax.erf(sc * math.sqrt(0.5)))
            w_ref[rows, tok] = (act * g).astype(w_ref.dtype)
            return carry

        lax.fori_loop(0, na, per_a, 0)

    o_ref[...] += jnp.dot(vt_ref[...], w_ref[...], preferred_element_type=jnp.float32)


def peer_experts(v, expert_u, expert_vt, s1a, e1a, s2, e2, thr, tt=512, nb=1024):
    T, D = v.shape
    N = expert_u.shape[0]
    na = nb // PEER_NKEYS
    H = PEER_HEADS
    return pl.pallas_call(
        _peer_expert_kernel,
        grid=(T // tt, N // nb),
        in_specs=[pl.BlockSpec((tt, D), lambda i, j: (i, 0)),
                  pl.BlockSpec((nb, D), lambda i, j: (j, 0)),
                  pl.BlockSpec((D, nb), lambda i, j: (0, j)),
                  pl.BlockSpec((na, H, tt), lambda i, j: (j, 0, i)),
                  pl.BlockSpec((na, H, tt), lambda i, j: (j, 0, i)),
                  pl.BlockSpec((H, PEER_NKEYS, tt), lambda i, j: (0, 0, i)),
                  pl.BlockSpec((H, PEER_NKEYS, tt), lambda i, j: (0, 0, i)),
                  pl.BlockSpec((H, tt), lambda i, j: (0, i))],
        out_specs=pl.BlockSpec((D, tt), lambda i, j: (0, i)),
        out_shape=jax.ShapeDtypeStruct((D, T), jnp.float32),
        scratch_shapes=[pltpu.VMEM((nb, tt), jnp.float32), pltpu.VMEM((nb, tt), jnp.bfloat16)],
        compiler_params=pltpu.CompilerParams(
            dimension_semantics=("parallel", "arbitrary"),
            vmem_limit_bytes=VMEM_LIMIT_BYTES),
    )(v, expert_u, expert_vt, s1a, e1a, s2, e2, thr)


def peer(u, w_q, sub_keys, expert_u, expert_v, tk=256, tt=512, nb=1024):
    ub = u.astype(jnp.bfloat16)
    q = pmm(ub, w_q.astype(jnp.bfloat16), out_dtype=jnp.bfloat16)
    s1, s2, e1, e2, thr = peer_topk(q, sub_keys, tk=tk)
    out_t = peer_experts(ub, expert_u.astype(jnp.bfloat16), expert_v.astype(jnp.bfloat16).T,
                         s1.transpose(1, 0, 2), e1.transpose(1, 0, 2), s2, e2, thr, tt=tt, nb=nb)
    return out_t.T


ROW_BLK = 256
STATE_MIN = -1e30

COL_QK, COL_V, COL_O = 0, 2 * ML_QK_W, 2 * ML_QK_W + ML_V_W
COL_HY = COL_O + ML_V_W
COL_BG = COL_HY + 3 * HY_W
COL_GATE = COL_BG + 2 * D_MODEL
IN_W_PAD = 16896


def _params(*sem):
    return pltpu.CompilerParams(dimension_semantics=sem, vmem_limit_bytes=VMEM_LIMIT_BYTES)


def _norm_mod_kernel(x_ref, g_ref, sc_ref, sh_ref, o_ref):
    x = x_ref[...]
    y = x * lax.rsqrt(jnp.mean(x * x, axis=-1, keepdims=True) + EPS)
    o_ref[...] = (y * g_ref[...] * (1.0 + sc_ref[0]) + sh_ref[0]).astype(o_ref.dtype)


def norm_mod(x_all, g, sc, sh, n_lat_blk):
    T, D = x_all.shape
    mod = pl.BlockSpec((1, 1, D), lambda i: (jnp.minimum(i // n_lat_blk, 1), 0, 0))
    return pl.pallas_call(
        _norm_mod_kernel,
        grid=(T // ROW_BLK,),
        in_specs=[pl.BlockSpec((ROW_BLK, D), lambda i: (i, 0)), pl.BlockSpec((1, D), lambda i: (0, 0)), mod, mod],
        out_specs=pl.BlockSpec((ROW_BLK, D), lambda i: (i, 0)),
        out_shape=jax.ShapeDtypeStruct((T, D), jnp.bfloat16),
        compiler_params=_params("parallel"),
    )(x_all, g[None], sc, sh)


def _fill_halo(buf_ref, prev_ref, cur_ref, next_ref):
    r = cur_ref.shape[0]
    buf_ref[0:r, :] = prev_ref[...]
    buf_ref[r:2 * r, :] = cur_ref[...]
    buf_ref[2 * r:3 * r, :] = next_ref[...]


def _qk_conv_kernel(prev_ref, cur_ref, next_ref, w_ref, o_ref, buf_ref, *, n_lat_blk, q_slabs):
    j = pl.program_id(0)
    i = pl.program_id(1)
    r, cw = cur_ref.shape
    _fill_halo(buf_ref, prev_ref, cur_ref, next_ref)
    local = lax.broadcasted_iota(jnp.int32, (r, cw), 0)
    n_lat = n_lat_blk * r

    def tap(off):
        return buf_ref[pl.ds(r + off, r), :]

    def finish(acc):
        y = acc * jax.nn.sigmoid(acc)
        scale = jnp.where(j < q_slabs, ML_DQK ** -0.5, 1.0)
        o_ref[...] = (y * scale).astype(o_ref.dtype)

    @pl.when(i < n_lat_blk)
    def _():
        g = i * r + local
        col = jnp.bitwise_and(g, GRID_W - 1)
        acc = jnp.zeros((r, cw), jnp.float32)
        for dr in (-1, 0, 1):
            for dc in (-1, 0, 1):
                t = tap(GRID_W * dr + dc)
                if dr == -1:
                    t = jnp.where(g >= GRID_W, t, 0.0)
                if dr == 1:
                    t = jnp.where(g < n_lat - GRID_W, t, 0.0)
                if dc == -1:
                    t = jnp.where(col >= 1, t, 0.0)
                if dc == 1:
                    t = jnp.where(col < GRID_W - 1, t, 0.0)
                acc = acc + t * w_ref[3 * (dr + 1) + dc + 1:3 * (dr + 1) + dc + 2, :]
        finish(acc)

    @pl.when(i >= n_lat_blk)
    def _():
        acc = tap(0) * w_ref[4:5, :]
        acc = acc + jnp.where(local >= 1, tap(-1), 0.0) * w_ref[3:4, :]
        acc = acc + jnp.where(local < r - 1, tap(1), 0.0) * w_ref[5:6, :]
        finish(acc)


def _halo_specs(cw, col0_blk, nblk, row_off=0):
    cur = lambda j, i: (i + row_off, col0_blk + j)
    prev = lambda j, i: (jnp.maximum(i + row_off - 1, 0), col0_blk + j)
    nxt = lambda j, i: (jnp.minimum(i + row_off + 1, nblk - 1), col0_blk + j)
    return [pl.BlockSpec((ROW_BLK, cw), prev), pl.BlockSpec((ROW_BLK, cw), cur), pl.BlockSpec((ROW_BLK, cw), nxt)]


def qk_conv(p, w9, n_lat_blk, cw=256):
    T = p.shape[0]
    nblk = T // ROW_BLK
    width = 2 * ML_QK_W
    return pl.pallas_call(
        functools.partial(_qk_conv_kernel, n_lat_blk=n_lat_blk, q_slabs=ML_QK_W // cw),
        grid=(width // cw, nblk),
        in_specs=_halo_specs(cw, COL_QK // cw, nblk) + [pl.BlockSpec((9, cw), lambda j, i: (0, j))],
        out_specs=pl.BlockSpec((ROW_BLK, cw), lambda j, i: (i, j)),
        out_shape=jax.ShapeDtypeStruct((T, width), jnp.bfloat16),
        scratch_shapes=[pltpu.VMEM((3 * ROW_BLK, cw), jnp.float32)],
        compiler_params=_params("parallel", "parallel"),
    )(p, p, p, w9)


def _seq_conv_kernel(prev_ref, cur_ref, next_ref, w_ref, o_ref, buf_ref, *, n_rows):
    i = pl.program_id(1)
    r, cw = cur_ref.shape
    _fill_halo(buf_ref, prev_ref, cur_ref, next_ref)
    g = i * r + lax.broadcasted_iota(jnp.int32, (r, cw), 0)
    acc = buf_ref[pl.ds(r, r), :] * w_ref[1:2, :]
    acc = acc + jnp.where(g >= 1, buf_ref[pl.ds(r - 1, r), :], 0.0) * w_ref[0:1, :]
    acc = acc + jnp.where(g < n_rows - 1, buf_ref[pl.ds(r + 1, r), :], 0.0) * w_ref[2:3, :]
    o_ref[...] = acc.astype(o_ref.dtype)


def hy_conv(p, w3, n_lat_blk, cw=512):
    nblk = p.shape[0] // ROW_BLK
    width = 3 * HY_W
    n_rows = n_lat_blk * ROW_BLK
    return pl.pallas_call(
        functools.partial(_seq_conv_kernel, n_rows=n_rows),
        grid=(width // cw, n_lat_blk),
        in_specs=_halo_specs(cw, COL_HY // cw, nblk) + [pl.BlockSpec((3, cw), lambda j, i: (0, j))],
        out_specs=pl.BlockSpec((ROW_BLK, cw), lambda j, i: (i, j)),
        out_shape=jax.ShapeDtypeStruct((n_rows, width), jnp.float32),
        scratch_shapes=[pltpu.VMEM((3 * ROW_BLK, cw), jnp.float32)],
        compiler_params=_params("parallel", "parallel"),
    )(p, p, p, w3)


def _mlstm_kernel(q_ref, k_ref, kt_ref, v_ref, g_ref, o_ref, c_ref, n_ref, m_ref):
    d = pl.program_id(0)
    step = pl.program_id(2)
    r = q_ref.shape[0]

    @pl.when(step == 0)
    def _():
        c_ref[...] = jnp.zeros_like(c_ref)
        n_ref[...] = jnp.zeros_like(n_ref)
        m_ref[...] = jnp.full_like(m_ref, STATE_MIN)

    ig = g_ref[0, 0, 0:1, :]
    fg = g_ref[0, 0, 1:2, :]
    t_i = lax.broadcasted_iota(jnp.int32, (r, r), 0)
    s_i = lax.broadcasted_iota(jnp.int32, (r, r), 1)
    sgn = jnp.where(d == 0, 1, -1)
    reads = (s_i - t_i) * sgn <= 0
    read_by = (t_i - s_i) * sgn <= 0
    eye = s_i == t_i
    f_col = jnp.sum(jnp.where(eye, fg, 0.0), axis=1, keepdims=True)
    cum_col = jnp.sum(jnp.where(reads, fg, 0.0), axis=1, keepdims=True)
    cum_row = jnp.sum(jnp.where(read_by, f_col, 0.0), axis=0, keepdims=True)
    total = jnp.sum(fg, axis=1, keepdims=True)
    m_old = m_ref[...]

    dmat = jnp.where(reads, (cum_col - cum_row) + ig, NEG_INF)
    inter = cum_col + m_old
    m_t = jnp.maximum(inter, jnp.max(dmat, axis=1, keepdims=True))
    w = jnp.exp(dmat - m_t)
    s_inter = jnp.exp(inter - m_t)
    q = q_ref[...]
    v = v_ref[...]
    qkw = jnp.dot(q, kt_ref[...], preferred_element_type=jnp.float32) * w
    qn = jnp.sum(q.astype(jnp.float32) * n_ref[...], axis=1, keepdims=True)
    den = jnp.sum(qkw, axis=1, keepdims=True) + s_inter * qn
    num = jnp.dot(qkw.astype(jnp.bfloat16), v.astype(jnp.bfloat16), preferred_element_type=jnp.float32) \
        + s_inter * jnp.dot(q, c_ref[...].astype(jnp.bfloat16), preferred_element_type=jnp.float32)
    o_ref[0] = num / jnp.maximum(jnp.abs(den), jnp.exp(-m_t))

    dec = (total - cum_row) + ig
    m_new = jnp.maximum(total + m_old, jnp.max(dec, axis=1, keepdims=True))
    wk_row = jnp.exp(dec - m_new)
    s_old = jnp.exp(total + m_old - m_new)
    wk_col = jnp.sum(jnp.where(eye, wk_row, 0.0), axis=1, keepdims=True)
    c_ref[...] = s_old * c_ref[...] + jnp.dot(kt_ref[...], (wk_col * v).astype(jnp.bfloat16),
                                              preferred_element_type=jnp.float32)
    n_ref[...] = s_old * n_ref[...] + jnp.sum(wk_col * k_ref[...].astype(jnp.float32), axis=0, keepdims=True)
    m_ref[...] = m_new


def mlstm_scan(qk, kt, p, gates, n_lat_blk):
    T = qk.shape[0]
    r = ROW_BLK

    def blk(d, s):
        lat = jnp.where(d == 0, s - 1, n_lat_blk - s)
        return jnp.where(s == 0, n_lat_blk, lat)

    return pl.pallas_call(
        _mlstm_kernel,
        grid=(2, ML_HEADS, n_lat_blk + 1),
        in_specs=[pl.BlockSpec((r, ML_DQK), lambda d, h, s: (blk(d, s), h)),
                  pl.BlockSpec((r, ML_DQK), lambda d, h, s: (blk(d, s), ML_HEADS + h)),
                  pl.BlockSpec((ML_DQK, r), lambda d, h, s: (h, blk(d, s))),
                  pl.BlockSpec((r, ML_DV), lambda d, h, s: (blk(d, s), COL_V // ML_DV + h)),
                  pl.BlockSpec((1, 1, 2, r), lambda d, h, s: (d, h, 0, blk(d, s)))],
        out_specs=pl.BlockSpec((1, r, ML_DV), lambda d, h, s: (d, blk(d, s), h)),
        out_shape=jax.ShapeDtypeStruct((2, T, ML_V_W), jnp.float32),
        scratch_shapes=[pltpu.VMEM((ML_DQK, ML_DV), jnp.float32), pltpu.VMEM((1, ML_DQK), jnp.float32),
                        pltpu.VMEM((1, 1), jnp.float32)],
        compiler_params=_params("parallel", "parallel", "arbitrary"),
    )(qk, qk, kt, p, gates)


def _ml_out_kernel(hf_ref, hb_ref, o_ref, g_ref, out_ref):
    for h in range(ML_HEADS):
        ch = slice(h * ML_DV, (h + 1) * ML_DV)
        hh = hf_ref[0, :, ch] + hb_ref[0, :, ch]
        hh = hh * lax.rsqrt(jnp.mean(hh * hh, axis=-1, keepdims=True) + EPS)
        out_ref[:, ch] = (hh * g_ref[:, ch] * jax.nn.sigmoid(o_ref[:, ch])).astype(out_ref.dtype)


def ml_out(h2, p, gain, n_lat_blk):
    r = ROW_BLK
    return pl.pallas_call(
        _ml_out_kernel,
        grid=(n_lat_blk,),
        in_specs=[pl.BlockSpec((1, r, ML_V_W), lambda i: (0, i, 0)),
                  pl.BlockSpec((1, r, ML_V_W), lambda i: (1, i, 0)),
                  pl.BlockSpec((r, ML_V_W), lambda i: (i, COL_O // ML_V_W)),
                  pl.BlockSpec((1, ML_V_W), lambda i: (0, 0))],
        out_specs=pl.BlockSpec((r, ML_V_W), lambda i: (i, 0)),
        out_shape=jax.ShapeDtypeStruct((n_lat_blk * r, ML_V_W), jnp.bfloat16),
        compiler_params=_params("parallel"),
    )(h2, h2, p, gain[None])


def _merge_kernel(hm_ref, hh_ref, wm_ref, wh_ref, gm_ref, gh_ref, o_ref):
    ym = jnp.dot(hm_ref[...], wm_ref[...], preferred_element_type=jnp.float32)
    yh = jnp.dot(hh_ref[...], wh_ref[...], preferred_element_type=jnp.float32)
    o_ref[...] = (jax.nn.sigmoid(gm_ref[...]) * ym + jax.nn.sigmoid(gh_ref[...]) * yh).astype(o_ref.dtype)


def merge_branches(h_ml, h_hy, w_pm, w_ph, p, tm=512, tn=1024):
    L, K = h_ml.shape
    D = w_pm.shape[1]
    act = pl.BlockSpec((tm, K), lambda j, i: (i, 0))
    wgt = pl.BlockSpec((K, tn), lambda j, i: (0, j))
    return pl.pallas_call(
        _merge_kernel,
        grid=(D // tn, L // tm),
        in_specs=[act, act, wgt, wgt,
                  pl.BlockSpec((tm, tn), lambda j, i: (i, COL_BG // tn + j)),
                  pl.BlockSpec((tm, tn), lambda j, i: (i, (COL_BG + D) // tn + j))],
        out_specs=pl.BlockSpec((tm, tn), lambda j, i: (i, j)),
        out_shape=jax.ShapeDtypeStruct((L, D), jnp.bfloat16),
        compiler_params=_params("parallel", "parallel"),
    )(h_ml, h_hy, w_pm, w_ph, p, p)


def _out_proj_kernel(y_ref, w_ref, x_ref, g1_ref, ng_ref, sc_ref, sh_ref, h_ref, v_ref):
    h = x_ref[...] + g1_ref[...] * jnp.dot(y_ref[...], w_ref[...], preferred_element_type=jnp.float32)
    h_ref[...] = h
    n = h * lax.rsqrt(jnp.mean(h * h, axis=-1, keepdims=True) + EPS)
    v_ref[...] = (n * ng_ref[...] * (1.0 + sc_ref[...]) + sh_ref[...]).astype(v_ref.dtype)


def out_proj(y, w_o, x, g1, ng, sc, sh, tm=256):
    L, D = x.shape
    row = pl.BlockSpec((tm, D), lambda i: (i, 0))
    vec = pl.BlockSpec((1, D), lambda i: (0, 0))
    return pl.pallas_call(
        _out_proj_kernel,
        grid=(L // tm,),
        in_specs=[row, pl.BlockSpec((D, D), lambda i: (0, 0)), row, vec, vec, vec, vec],
        out_specs=[row, row],
        out_shape=[jax.ShapeDtypeStruct((L, D), jnp.float32), jax.ShapeDtypeStruct((L, D), jnp.bfloat16)],
        compiler_params=_params("parallel"),
    )(y, w_o, x, g1, ng, sc, sh)


def _final_kernel(h_ref, p_ref, g2_ref, fg_ref, o_ref):
    h = h_ref[...] + g2_ref[...] * p_ref[...]
    o_ref[...] = h * lax.rsqrt(jnp.mean(h * h, axis=-1, keepdims=True) + EPS) * fg_ref[...]


def final_norm(h, peer_out, g2, fg, tm=256):
    L, D = h.shape
    row = pl.BlockSpec((tm, D), lambda i: (i, 0))
    vec = pl.BlockSpec((1, D), lambda i: (0, 0))
    return pl.pallas_call(
        _final_kernel,
        grid=(L // tm,),
        in_specs=[row, row, vec, vec],
        out_specs=row,
        out_shape=jax.ShapeDtypeStruct((L, D), jnp.float32),
        compiler_params=_params("parallel"),
    )(h, peer_out, g2, fg)


def token_mixer_inputs(x, ctx, mods, norm_g, w_in):
    L = x.shape[0]
    assert ctx.shape[0] == ROW_BLK and L % ROW_BLK == 0
    n_lat_blk = L // ROW_BLK
    x_all = jnp.concatenate([x, ctx], axis=0)
    sh1, sc1 = mods[:, 0], mods[:, 1]
    u = norm_mod(x_all, norm_g, sc1[:, None, :], sh1[:, None, :], n_lat_blk)
    n_in = w_in.shape[1]
    w_r = jnp.concatenate([w_in[:, :COL_O + ML_V_W], w_in[:, COL_O + ML_V_W + 4 * ML_HEADS:],
                           w_in[:, COL_O + ML_V_W:COL_O + ML_V_W + 4 * ML_HEADS],
                           jnp.zeros((w_in.shape[0], IN_W_PAD - n_in), w_in.dtype)], axis=1).astype(jnp.bfloat16)
    return pmm(u, w_r, tm=768, tn=1536), n_lat_blk


def kernel(x, c, ctx, c_ctx, w_mod, b_mod, norm1_g, norm2_g, final_g, w_in, ml_conv_w, ml_gate_b, ml_norm_g, hy_conv_w, hy_w1, hy_b1, hy_w2, hy_b2, hy_w3, hy_freq, hy_bias, w_proj_ml, w_proj_hy, w_out, peer_wq, peer_keys, peer_u, peer_v):
    l = 0
    L, D = x.shape[1:]
    bf = jnp.bfloat16
    s_rows = jnp.zeros((8, D), jnp.float32).at[0].set(jax.nn.silu(c[0])).at[1].set(jax.nn.silu(c_ctx))
    mods = (pmm(s_rows, w_mod[l], tm=8, tn=1024)[:2] + b_mod[l]).reshape(2, 6, D)

    p, n_lat_blk = token_mixer_inputs(x[0], ctx[0], mods, norm1_g[l], w_in[l])
    T = p.shape[0]

    qk = qk_conv(p, ml_conv_w[l].reshape(9, 2 * ML_QK_W), n_lat_blk)
    kt = qk[:, ML_QK_W:].T
    gt = p[:, COL_GATE:COL_GATE + 4 * ML_HEADS].reshape(T, 4, ML_HEADS) + ml_gate_b[l]
    gates = jnp.stack([jnp.stack([gt[:, 0], jax.nn.log_sigmoid(gt[:, 1])]),
                       jnp.stack([gt[:, 2], jax.nn.log_sigmoid(gt[:, 3])])]).transpose(0, 3, 1, 2)
    h2 = mlstm_scan(qk, kt, p, gates, n_lat_blk)
    h_ml = ml_out(h2, p, ml_norm_g[l], n_lat_blk)

    xs = hy_conv(p, hy_conv_w[l], n_lat_blk)
    taps = hyena_filter_taps(L, hy_w1[l], hy_b1[l], hy_w2[l], hy_b2[l], hy_w3[l], hy_freq[l])
    h_hy = hyena(xs, taps, hy_bias[l])

    y = merge_branches(h_ml, h_hy.astype(bf), w_proj_ml[l].astype(bf), w_proj_hy[l].astype(bf), p)
    h_lat, v_lat = out_proj(y, w_out[l].astype(bf), x[0], mods[0:1, 2], norm2_g[l][None], mods[0:1, 4], mods[0:1, 3])
    peer_out = peer(v_lat, peer_wq[l], peer_keys[l], peer_u[l], peer_v[l])
    return final_norm(h_lat, peer_out, mods[0:1, 5], final_g[None])[None]


def _old_kernel(x, c, ctx, c_ctx, w_mod, b_mod, norm1_g, norm2_g, final_g, w_in, ml_conv_w, ml_gate_b, ml_norm_g, hy_conv_w, hy_w1, hy_b1, hy_w2, hy_b2, hy_w3, hy_freq, hy_bias, w_proj_ml, w_proj_hy, w_out, peer_wq, peer_keys, peer_u, peer_v):
    B, L, D = x.shape
    rows = L // GRID_W
    l = 0
    s_lat = jax.nn.silu(c)[:, None, :]
    s_ctx = jax.nn.silu(c_ctx)
    h_lat, h_ctx = x, ctx
    sh1, sc1, g1, sh2, sc2, g2 = jnp.split(s_lat @ w_mod[l] + b_mod[l], 6, axis=-1)
    csh1, csc1, cg1, csh2, csc2, cg2 = jnp.split(s_ctx @ w_mod[l] + b_mod[l], 6, axis=-1)

    u_lat = rmsnorm(h_lat, norm1_g[l]) * (1.0 + sc1) + sh1
    u_ctx = rmsnorm(h_ctx, norm1_g[l]) * (1.0 + csc1) + csh1
    w_in_p = jnp.pad(w_in[l], ((0, 0), (0, 16896 - w_in.shape[-1])))
    p_lat = pmm(u_lat[0], w_in_p)[None, :, :w_in.shape[-1]]
    p_ctx = pmm(u_ctx[0], w_in_p)[None, :, :w_in.shape[-1]]
    qk_l, v_l, o_l, gt_l, hy_l, bg_l = jnp.split(p_lat, IN_SPLIT, axis=-1)
    qk_c, v_c, o_c, gt_c, hy_c, bg_c = jnp.split(p_ctx, IN_SPLIT, axis=-1)

    qk_l = jax.nn.silu(dwconv_grid(qk_l, ml_conv_w[l], rows))
    qk_c = jax.nn.silu(dwconv1d(qk_c, ml_conv_w[l][1]))
    q_l, k_l, vh_l, gf_l, gb_l = mlstm_heads(qk_l, v_l, gt_l, ml_gate_b[l])
    q_c, k_c, vh_c, gf_c, gb_c = mlstm_heads(qk_c, v_c, gt_c, ml_gate_b[l])
    st_f = mlstm_final_state(k_c, vh_c, gf_c[0], gf_c[1])
    st_b = mlstm_final_state(_flip(k_c), _flip(vh_c), _flip(gb_c[0]), _flip(gb_c[1]))
    h_ml_l = mlstm_out(mlstm_bidir(q_l, k_l, vh_l, gf_l, gb_l, st_f, st_b), o_l, ml_norm_g[l])

    taps = hyena_filter_taps(L, hy_w1[l], hy_b1[l], hy_w2[l], hy_b2[l], hy_w3[l], hy_freq[l])
    h_hy_l = hyena(dwconv1d(hy_l, hy_conv_w[l])[0], taps, hy_bias[l])[None]
    mix_l = merge(h_ml_l, h_hy_l, bg_l, w_proj_ml[l], w_proj_hy[l], w_out[l])

    h_lat = h_lat + g1 * mix_l
    v_lat = rmsnorm(h_lat, norm2_g[l]) * (1.0 + sc2) + sh2
    h_lat = h_lat + g2 * peer(v_lat[0], peer_wq[l], peer_keys[l], peer_u[l], peer_v[l])[None]
    return rmsnorm(h_lat, final_g)
```

```python
import functools
import math
import numpy as np
import jax
import jax.numpy as jnp
from jax import lax
from jax.experimental import pallas as pl
from jax.experimental.pallas import tpu as pltpu

D_MODEL = 2048
GRID_W = 64
EPS = 1e-6

ML_HEADS = 8
ML_DQK = D_MODEL // 16
ML_DV = D_MODEL // 8
ML_QK_W = ML_HEADS * ML_DQK
ML_V_W = ML_HEADS * ML_DV
ML_CHUNK = 64

HY_W = D_MODEL
HY_ORDER = 2
HY_BANDS = 8
HY_SHIFT = 0.05
HY_MIN_DECAY = math.log(1e-2) / 1.5
HY_MAX_DECAY = math.log(1e-2) / 0.3

PEER_HEADS = 8
PEER_NKEYS = 128
PEER_DKEY = 256
PEER_TOPK = 16
PEER_BLOCK = 128

IN_SPLIT = (2 * ML_QK_W,
            2 * ML_QK_W + ML_V_W,
            2 * ML_QK_W + 2 * ML_V_W,
            2 * ML_QK_W + 2 * ML_V_W + 4 * ML_HEADS,
            2 * ML_QK_W + 2 * ML_V_W + 4 * ML_HEADS + 3 * HY_W)

VMEM_LIMIT_BYTES = 56 * 1024 * 1024


def _mm_kernel(a_ref, b_ref, o_ref):
    o_ref[...] = jnp.dot(a_ref[...].astype(jnp.bfloat16), b_ref[...].astype(jnp.bfloat16),
                         preferred_element_type=jnp.float32).astype(o_ref.dtype)


def pmm(a, b, tm=512, tn=512, out_dtype=jnp.float32):
    M, K = a.shape
    _, N = b.shape
    tm = min(tm, M)
    tn = min(tn, N)
    assert M % tm == 0 and N % tn == 0, (M, N, tm, tn)
    return pl.pallas_call(
        _mm_kernel,
        grid=(N // tn, M // tm),
        in_specs=[pl.BlockSpec((tm, K), lambda j, i: (i, 0)),
                  pl.BlockSpec((K, tn), lambda j, i: (0, j))],
        out_specs=pl.BlockSpec((tm, tn), lambda j, i: (i, j)),
        out_shape=jax.ShapeDtypeStruct((M, N), out_dtype),
        compiler_params=pltpu.CompilerParams(
            dimension_semantics=("parallel", "parallel"),
            vmem_limit_bytes=VMEM_LIMIT_BYTES),
    )(a, b)


def rmsnorm(x, g):
    xf = x.astype(jnp.float32)
    y = xf * lax.rsqrt(jnp.mean(xf * xf, axis=-1, keepdims=True) + EPS)
    return y.astype(x.dtype) * g


def dwconv1d(x, w):
    K = w.shape[0]
    L = x.shape[1]
    pad = K // 2
    xp = jnp.pad(x, ((0, 0), (pad, pad), (0, 0)))
    y = xp[:, 0:L] * w[0]
    for j in range(1, K):
        y = y + xp[:, j:j + L] * w[j]
    return y


def dwconv_grid(x, w, rows):
    B, L, C = x.shape
    img = x.reshape(B, rows, GRID_W, C)
    y = lax.conv_general_dilated(img, w[:, :, None, :], (1, 1), 'SAME',
                                 dimension_numbers=('NHWC', 'HWIO', 'NHWC'),
                                 feature_group_count=C)
    return y.reshape(B, L, C)


def _flip(a):
    return jnp.flip(a, axis=2)


def mlstm_heads(qk, v, gates, gate_b):
    B, L, _ = v.shape
    q, k = jnp.split(qk, 2, axis=-1)
    q = q.reshape(B, L, ML_HEADS, ML_DQK).transpose(0, 2, 1, 3).astype(jnp.float32) * (ML_DQK ** -0.5)
    k = k.reshape(B, L, ML_HEADS, ML_DQK).transpose(0, 2, 1, 3).astype(jnp.float32)
    vh = v.reshape(B, L, ML_HEADS, ML_DV).transpose(0, 2, 1, 3).astype(jnp.float32)
    g = gates.astype(jnp.float32).reshape(B, L, 4, ML_HEADS).transpose(2, 0, 3, 1) \
        + gate_b.astype(jnp.float32)[:, None, :, None]
    g_fwd = (g[0], jax.nn.log_sigmoid(g[1]))
    g_bwd = (g[2], jax.nn.log_sigmoid(g[3]))
    return q, k, vh, g_fwd, g_bwd


def mlstm_final_state(k, v, log_i, log_f):
    b = jnp.cumsum(log_f, axis=-1)
    dec = b[..., -1:] - b + log_i
    m = jnp.max(dec, axis=-1)
    w = jnp.exp(dec - m[..., None])
    C = jnp.einsum('bhs,bhsd,bhsv->bhdv', w, k, v)
    n = jnp.einsum('bhs,bhsd->bhd', w, k)
    return (C, n, m)


def mlstm_chunkwise(q, k, v, log_i, log_f, state):
    B, H, L, _ = q.shape
    T = ML_CHUNK
    nc = L // T

    def to_chunks(a):
        return jnp.moveaxis(a.reshape(a.shape[:2] + (nc, T) + a.shape[3:]), 2, 0)

    causal = jnp.tril(jnp.ones((T, T), dtype=bool))

    def step(carry, inp):
        C, n, m = carry
        qb, kb, vb, ib, fb = inp
        b = jnp.cumsum(fb, axis=-1)
        D = b[..., :, None] - b[..., None, :] + ib[..., None, :]
        D = jnp.where(causal, D, -jnp.inf)
        inter = b + m[..., None]
        m_t = jnp.maximum(inter, jnp.max(D, axis=-1))
        w = jnp.exp(D - m_t[..., None])
        s_inter = jnp.exp(inter - m_t)
        qk = jnp.einsum('bhtd,bhsd->bhts', qb, kb) * w
        num = jnp.einsum('bhts,bhsv->bhtv', qk, vb) \
            + s_inter[..., None] * jnp.einsum('bhtd,bhdv->bhtv', qb, C)
        den = jnp.sum(qk, axis=-1) + s_inter * jnp.einsum('bhtd,bhd->bht', qb, n)
        h = num / jnp.maximum(jnp.abs(den), jnp.exp(-m_t))[..., None]
        bT = b[..., -1]
        dec = bT[..., None] - b + ib
        m_new = jnp.maximum(bT + m, jnp.max(dec, axis=-1))
        wk = jnp.exp(dec - m_new[..., None])
        s_old = jnp.exp(bT + m - m_new)
        C_new = s_old[..., None, None] * C + jnp.einsum('bhs,bhsd,bhsv->bhdv', wk, kb, vb)
        n_new = s_old[..., None] * n + jnp.einsum('bhs,bhsd->bhd', wk, kb)
        return (C_new, n_new, m_new), h

    _, h = lax.scan(step, state, (to_chunks(q), to_chunks(k), to_chunks(v),
                                  to_chunks(log_i), to_chunks(log_f)))
    return jnp.moveaxis(h, 0, 2).reshape(B, H, L, v.shape[-1])


def mlstm_bidir(q, k, v, g_fwd, g_bwd, st_f, st_b):
    h_f = mlstm_chunkwise(q, k, v, g_fwd[0], g_fwd[1], st_f)
    h_b = mlstm_chunkwise(_flip(q), _flip(k), _flip(v), _flip(g_bwd[0]), _flip(g_bwd[1]), st_b)
    return h_f + _flip(h_b)


def mlstm_out(h, o, g):
    B, H, L, dv = h.shape
    h = h * lax.rsqrt(jnp.mean(h * h, axis=-1, keepdims=True) + EPS)
    h = h.transpose(0, 2, 1, 3).reshape(B, L, H * dv).astype(o.dtype)
    return h * g * jax.nn.sigmoid(o)


FFT_R = 128
FFT_N = FFT_R * FFT_R
HY_FEAT_PAD = 32
HIGHEST = lax.Precision.HIGHEST


def _dft_tables():
    r = np.arange(FFT_R)
    ang = 2.0 * np.pi * np.outer(r, r) / FFT_R
    c, s = np.cos(ang), np.sin(ang)
    first = np.concatenate([c, -s], axis=0)
    mid_fwd = np.block([[c, s], [-s, c]])
    mid_inv = np.block([[c, -s], [s, c]])
    last = np.concatenate([c, -s], axis=1) / FFT_N
    tw = 2.0 * np.pi * np.outer(r, r) / FFT_N
    f32 = lambda a: jnp.asarray(a, jnp.float32)
    return f32(first), f32(mid_fwd), f32(mid_inv), f32(last), f32(np.cos(tw)), f32(np.sin(tw))


def _fft_mid_kernel(a_ref, tc_ref, ts_ref, mf_ref, mi_ref, *rest, inverse):
    if inverse:
        g_ref, o_ref, b_ref = rest
    else:
        o_ref, b_ref = rest
    cb = a_ref.shape[-1]
    tc = tc_ref[0]
    ts = ts_ref[0]
    for lt in range(cb // LANES):
        ch = slice(lt * LANES, (lt + 1) * LANES)
        ar = a_ref[0, 0, :, ch]
        ai = a_ref[1, 0, :, ch]
        b_ref[0:FFT_R, ch] = (ar * tc + ai * ts).astype(b_ref.dtype)
        b_ref[FFT_R:, ch] = (ai * tc - ar * ts).astype(b_ref.dtype)
    x = jnp.dot(mf_ref[...], b_ref[...], preferred_element_type=jnp.float32)
    if not inverse:
        o_ref[0, 0] = x[:FFT_R].astype(o_ref.dtype)
        o_ref[1, 0] = x[FFT_R:].astype(o_ref.dtype)
        return
    xr, xi = x[:FFT_R], x[FFT_R:]
    gr = g_ref[0, 0].astype(jnp.float32)
    gi = g_ref[1, 0].astype(jnp.float32)
    b_ref[0:FFT_R, :] = (xr * gr - xi * gi).astype(b_ref.dtype)
    b_ref[FFT_R:, :] = (xr * gi + xi * gr).astype(b_ref.dtype)
    q = jnp.dot(mi_ref[...], b_ref[...], preferred_element_type=jnp.float32)
    for lt in range(cb // LANES):
        ch = slice(lt * LANES, (lt + 1) * LANES)
        qr = q[:FFT_R, ch]
        qi = q[FFT_R:, ch]
        o_ref[0, 0, :, ch] = (qr * tc - qi * ts).astype(o_ref.dtype)
        o_ref[1, 0, :, ch] = (qi * tc + qr * ts).astype(o_ref.dtype)


def fft_mid(a, tc, ts, mid_fwd, mid_inv, g=None, cb=2048, out_dtype=jnp.float32):
    C = a.shape[-1]
    cb = min(cb, C)
    inverse = g is not None
    blk = pl.BlockSpec((2, 1, FFT_R, cb), lambda k, j: (0, k, 0, j))
    tw = pl.BlockSpec((1, FFT_R, LANES), lambda k, j: (k, 0, 0))
    mat = pl.BlockSpec((2 * FFT_R, 2 * FFT_R), lambda k, j: (0, 0))
    return pl.pallas_call(
        functools.partial(_fft_mid_kernel, inverse=inverse),
        grid=(FFT_R, C // cb),
        in_specs=[blk, tw, tw, mat, mat] + ([blk] if inverse else []),
        out_specs=blk,
        out_shape=jax.ShapeDtypeStruct(a.shape, out_dtype),
        scratch_shapes=[pltpu.VMEM((2 * FFT_R, cb), jnp.bfloat16)],
        compiler_params=pltpu.CompilerParams(
            dimension_semantics=("parallel", "parallel"),
            vmem_limit_bytes=VMEM_LIMIT_BYTES),
    )(a, tc, ts, mid_fwd.astype(jnp.bfloat16), mid_inv.astype(jnp.bfloat16), *([g] if inverse else []))


FFT_GRP = 32


def _fft_first_kernel(x_ref, f_ref, o_ref):
    rows, cb = x_ref.shape
    n1 = rows // FFT_R
    kk = f_ref.shape[0]
    fb = jnp.broadcast_to(f_ref[...][None], (FFT_GRP, kk, n1))
    for g in range(FFT_R // FFT_GRP):
        n2 = slice(g * FFT_GRP, (g + 1) * FFT_GRP)
        xt = jnp.swapaxes(x_ref[...].reshape(n1, FFT_R, cb)[:, n2, :], 0, 1).astype(jnp.bfloat16)
        a = lax.dot_general(fb, xt, (((2,), (1,)), ((0,), (0,))), preferred_element_type=jnp.float32)
        o_ref[:, n2, :] = jnp.swapaxes(a, 0, 1).astype(o_ref.dtype)


def fft_first(x, rows, row_blk, col0, width, first, cb=128):
    n1 = rows // FFT_R
    return pl.pallas_call(
        _fft_first_kernel,
        grid=(width // cb,),
        in_specs=[pl.BlockSpec((rows, cb), lambda j: (row_blk, col0 // cb + j)),
                  pl.BlockSpec((2 * FFT_R, n1), lambda j: (0, 0))],
        out_specs=pl.BlockSpec((2 * FFT_R, FFT_R, cb), lambda j: (0, 0, j)),
        out_shape=jax.ShapeDtypeStruct((2 * FFT_R, FFT_R, width), jnp.bfloat16),
        compiler_params=pltpu.CompilerParams(
            dimension_semantics=("parallel",), vmem_limit_bytes=VMEM_LIMIT_BYTES),
    )(x, first[:, :n1].astype(jnp.bfloat16))


def _fft_last_kernel(q_ref, f_ref, z_ref, x_ref, b_ref, o_ref):
    kk, _, cb = q_ref.shape
    n1 = f_ref.shape[0]
    fb = jnp.broadcast_to(f_ref[...][None], (FFT_GRP, n1, kk))
    for g in range(FFT_R // FFT_GRP):
        qt = jnp.swapaxes(q_ref[:, g * FFT_GRP:(g + 1) * FFT_GRP, :].astype(jnp.float32), 0, 1)
        y = lax.dot_general(fb, qt.astype(jnp.bfloat16), (((2,), (1,)), ((0,), (0,))),
                            preferred_element_type=jnp.float32)
        yt = jnp.swapaxes(y, 0, 1)
        for i in range(n1):
            rows = slice(i * FFT_R + g * FFT_GRP, i * FFT_R + (g + 1) * FFT_GRP)
            o_ref[rows, :] = (x_ref[rows, :] * (yt[i] + z_ref[rows, :] * b_ref[...])).astype(o_ref.dtype)


def fft_last(q, last, z, z_col0, xg, xg_col0, bias, out_dtype, cb=128):
    C = q.shape[-1]
    L = z.shape[0]
    n1 = L // FFT_R
    return pl.pallas_call(
        _fft_last_kernel,
        grid=(C // cb,),
        in_specs=[pl.BlockSpec((2 * FFT_R, FFT_R, cb), lambda j: (0, 0, j)),
                  pl.BlockSpec((n1, 2 * FFT_R), lambda j: (0, 0)),
                  pl.BlockSpec((L, cb), lambda j: (0, z_col0 // cb + j)),
                  pl.BlockSpec((L, cb), lambda j: (0, xg_col0 // cb + j)),
                  pl.BlockSpec((1, cb), lambda j: (0, j))],
        out_specs=pl.BlockSpec((L, cb), lambda j: (0, j)),
        out_shape=jax.ShapeDtypeStruct((L, C), out_dtype),
        compiler_params=pltpu.CompilerParams(
            dimension_semantics=("parallel",), vmem_limit_bytes=VMEM_LIMIT_BYTES),
    )(q, last[:n1].astype(jnp.bfloat16), z, xg, bias)


def _hyena_filter_kernel(ft_ref, tn_ref, w1_ref, b1_ref, w2_ref, b2_ref, fr_ref, w3_ref, dl_ref, o_ref):
    pre = jnp.dot(ft_ref[...], w1_ref[...], preferred_element_type=jnp.float32, precision=HIGHEST)
    hdn = jnp.sin(fr_ref[...] * (pre + b1_ref[...]))
    pre = jnp.dot(hdn, w2_ref[...], preferred_element_type=jnp.float32, precision=HIGHEST)
    hdn = jnp.sin(fr_ref[...] * (pre + b2_ref[...]))
    filt = jnp.dot(hdn.astype(jnp.bfloat16), w3_ref[0].astype(jnp.bfloat16), preferred_element_type=jnp.float32)
    tn = tn_ref[...]
    cols = o_ref.shape[-1]
    for lt in range(cols // LANES):
        ch = slice(lt * LANES, (lt + 1) * LANES)
        window = jnp.exp(-tn * dl_ref[:, ch]) + HY_SHIFT
        o_ref[0, :, ch] = filt[:, ch] * window


def hyena_filter_taps(L, w1, b1, w2, b2, w3, freq, tr=512):
    assert FFT_N == 2 * L
    n = np.arange(FFT_N)
    t = np.where(n < L, n, (FFT_N - n) % L).astype(np.float32)
    tnorm = t / np.float32(L)
    bands = np.linspace(1e-4, HY_BANDS - 1, HY_BANDS, dtype=np.float32)
    ang = (np.float32(2.0 * math.pi / L) * t[:, None] * bands[None, :]).astype(np.float64)
    feats = np.concatenate([tnorm[:, None], np.cos(ang), -np.sin(ang)], axis=-1)
    feats = np.pad(feats, ((0, 0), (0, HY_FEAT_PAD - feats.shape[1]))).astype(np.float32)
    tmark = np.broadcast_to(tnorm[:, None], (FFT_N, LANES))
    deltas = np.abs(np.linspace(HY_MIN_DECAY, HY_MAX_DECAY, HY_W, dtype=np.float32))[None, :]
    w1p = jnp.pad(w1, ((0, HY_FEAT_PAD - w1.shape[0]), (0, 0)))
    ffn = w2.shape[0]
    w3r = w3.reshape(ffn, 2 * HY_ORDER, HY_W).transpose(1, 0, 2)
    half = L // tr
    row = lambda o, r: (r, 0)
    fix = lambda o, r: (0, 0)
    return pl.pallas_call(
        _hyena_filter_kernel,
        grid=(HY_ORDER, FFT_N // tr),
        in_specs=[pl.BlockSpec((tr, HY_FEAT_PAD), row),
                  pl.BlockSpec((tr, LANES), row),
                  pl.BlockSpec((HY_FEAT_PAD, ffn), fix),
                  pl.BlockSpec((1, ffn), fix),
                  pl.BlockSpec((ffn, ffn), fix),
                  pl.BlockSpec((1, ffn), fix),
                  pl.BlockSpec((1, ffn), fix),
                  pl.BlockSpec((1, ffn, HY_W), lambda o, r: ((r // half) * HY_ORDER + o, 0, 0)),
                  pl.BlockSpec((1, HY_W), fix)],
        out_specs=pl.BlockSpec((1, tr, HY_W), lambda o, r: (o, r, 0)),
        out_shape=jax.ShapeDtypeStruct((HY_ORDER, FFT_N, HY_W), jnp.float32),
        compiler_params=pltpu.CompilerParams(
            dimension_semantics=("parallel", "parallel"), vmem_limit_bytes=VMEM_LIMIT_BYTES),
    )(jnp.asarray(feats), jnp.asarray(tmark), w1p, b1[None], w2, b2[None], freq[None], w3r, jnp.asarray(deltas))


def hyena(xs, taps, bias):
    L = xs.shape[0]
    C = xs.shape[1] // 3
    first, mid_fwd, mid_inv, last, tcos, tsin = _dft_tables()
    tc = jnp.broadcast_to(tcos[:, :, None], (FFT_R, FFT_R, LANES))
    ts = jnp.broadcast_to(tsin[:, :, None], (FFT_R, FFT_R, LANES))
    shape4 = (2, FFT_R, FFT_R, C)
    bf = jnp.bfloat16
    taps2 = taps.reshape(HY_ORDER * FFT_N, C)
    z, z_col0 = xs, 2 * C
    for o in range(HY_ORDER):
        ga = fft_first(taps2, FFT_N, o, 0, C, first)
        g = fft_mid(ga.reshape(shape4), tc, ts, mid_fwd, mid_inv, out_dtype=bf)
        a = fft_first(z, L, 0, z_col0, C, first)
        q = fft_mid(a.reshape(shape4), tc, ts, mid_fwd, mid_inv, g=g, out_dtype=bf)
        bias_eff = (bias[o] + taps[o, L])[None, :]
        z = fft_last(q.reshape(2 * FFT_R, FFT_R, C), last, z, z_col0, xs, o * C, bias_eff,
                     jnp.float32 if o + 1 < HY_ORDER else bf)
        z_col0 = 0
    return z


def merge(h_ml, h_hy, bg, w_pm, w_ph, w_o):
    g_ml, g_hy = jnp.split(bg, 2, axis=-1)
    y = jax.nn.sigmoid(g_ml) * pmm(h_ml[0], w_pm)[None] + jax.nn.sigmoid(g_hy) * pmm(h_hy[0], w_ph)[None]
    return pmm(y[0], w_o)[None]


LANES = 128
NEG_INF = float('-inf')
_CAND_ROWS = tuple(PEER_TOPK // (i + 1) for i in range(PEER_TOPK))


def _extract_top(s, rounds):
    rows = s.shape[0]
    iota = lax.broadcasted_iota(jnp.int32, s.shape, 0).astype(jnp.float32)
    tops = []
    for _ in range(rounds):
        m = jnp.max(s, axis=0, keepdims=True)
        first = jnp.min(jnp.where(s == m, iota, float(rows)), axis=0, keepdims=True)
        s = jnp.where(iota == first, NEG_INF, s)
        tops.append(m)
    return tops


def _peer_topk_kernel(q_ref, k_ref, s1_ref, s2_ref, e1_ref, e2_ref, thr_ref):
    tk = q_ref.shape[0]
    for u in range(tk // LANES):
        tok = slice(u * LANES, (u + 1) * LANES)
        s = []
        tops = []
        for p in range(2):
            qp = q_ref[tok, p * PEER_NKEYS:(p + 1) * PEER_NKEYS]
            sp = lax.dot_general(k_ref[0, p], qp, (((1,), (1,)), ((), ())),
                                 preferred_element_type=jnp.float32)
            s.append(sp)
            tops.append(_extract_top(sp, PEER_TOPK))
        top2a = jnp.concatenate(tops[1][:8], axis=0)
        top2b = jnp.concatenate(tops[1][8:], axis=0)
        row = lax.broadcasted_iota(jnp.int32, (8, LANES), 0)
        cands = [tops[0][0] + top2a, tops[0][0] + top2b]
        for i in range(1, PEER_TOPK):
            cands.append(jnp.where(row < _CAND_ROWS[i], tops[0][i] + top2a, NEG_INF))
        best = _extract_top(jnp.concatenate(cands, axis=0), PEER_TOPK)
        z = jnp.ones_like(best[0])
        for b in best[1:]:
            z = z + jnp.exp(b - best[0])
        s1_ref[0, :, tok] = s[0]
        s2_ref[0, :, tok] = s[1]
        e1_ref[0, :, tok] = jnp.exp(s[0] - tops[0][0]) / z
        e2_ref[0, :, tok] = jnp.exp(s[1] - tops[1][0])
        thr_ref[0, :, tok] = best[-1]


def peer_topk(q, sub_keys, tk=256):
    T = q.shape[0]
    H = PEER_HEADS
    big = jax.ShapeDtypeStruct((H, PEER_NKEYS, T), jnp.float32)
    big_spec = pl.BlockSpec((1, PEER_NKEYS, tk), lambda i, h: (h, 0, i))
    s1, s2, e1, e2, thr = pl.pallas_call(
        _peer_topk_kernel,
        grid=(T // tk, H),
        in_specs=[pl.BlockSpec((tk, PEER_DKEY), lambda i, h: (i, h)),
                  pl.BlockSpec((1, 2, PEER_NKEYS, PEER_DKEY // 2), lambda i, h: (h, 0, 0, 0))],
        out_specs=[big_spec, big_spec, big_spec, big_spec,
                   pl.BlockSpec((1, 1, tk), lambda i, h: (h, 0, i))],
        out_shape=[big, big, big, big, jax.ShapeDtypeStruct((H, 1, T), jnp.float32)],
        compiler_params=pltpu.CompilerParams(
            dimension_semantics=("parallel", "parallel"),
            vmem_limit_bytes=VMEM_LIMIT_BYTES),
    )(q, sub_keys.astype(jnp.bfloat16))
    return s1, s2, e1, e2, thr.reshape(H, T)


def _peer_expert_kernel(v_ref, u_ref, vt_ref, s1a_ref, e1a_ref, s2_ref, e2_ref, thr_ref, o_ref, sc_ref, w_ref):
    j = pl.program_id(1)
    tt = v_ref.shape[0]
    na = u_ref.shape[0] // PEER_NKEYS

    @pl.when(j == 0)
    def _():
        o_ref[...] = jnp.zeros_like(o_ref)

    sc_ref[...] = lax.dot_general(u_ref[...], v_ref[...], (((1,), (1,)), ((), ())),
                                  preferred_element_type=jnp.float32)

    for u in range(tt // LANES):
        tok = slice(u * LANES, (u + 1) * LANES)

        def per_a(al, carry):
            g = jnp.zeros((PEER_NKEYS, LANES), jnp.float32)
            for h in range(PEER_HEADS):
                s1row = s1a_ref[al, h:h + 1, tok]
                e1row = e1a_ref[al, h:h + 1, tok]
                keep = (s1row + s2_ref[h, :, tok]) >= thr_ref[h:h + 1, tok]
                g = g + jnp.where(keep, e1row * e2_ref[h, :, tok], 0.0)
            rows = pl.ds(pl.multiple_of(al * PEER_NKEYS, PEER_NKEYS), PEER_NKEYS)
            sc = sc_ref[rows, tok]
            act = 0.5 * sc * (1.0 + lax.erf(sc * math.sqrt(0.5)))
            w_ref[rows, tok] = (act * g).astype(w_ref.dtype)
            return carry

        lax.fori_loop(0, na, per_a, 0)

    o_ref[...] += jnp.dot(vt_ref[...], w_ref[...], preferred_element_type=jnp.float32)


def peer_experts(v, expert_u, expert_vt, s1a, e1a, s2, e2, thr, tt=512, nb=1024):
    T, D = v.shape
    N = expert_u.shape[0]
    na = nb // PEER_NKEYS
    H = PEER_HEADS
    return pl.pallas_call(
        _peer_expert_kernel,
        grid=(T // tt, N // nb),
        in_specs=[pl.BlockSpec((tt, D), lambda i, j: (i, 0)),
                  pl.BlockSpec((nb, D), lambda i, j: (j, 0)),
                  pl.BlockSpec((D, nb), lambda i, j: (0, j)),
                  pl.BlockSpec((na, H, tt), lambda i, j: (j, 0, i)),
                  pl.BlockSpec((na, H, tt), lambda i, j: (j, 0, i)),
                  pl.BlockSpec((H, PEER_NKEYS, tt), lambda i, j: (0, 0, i)),
                  pl.BlockSpec((H, PEER_NKEYS, tt), lambda i, j: (0, 0, i)),
                  pl.BlockSpec((H, tt), lambda i, j: (0, i))],
        out_specs=pl.BlockSpec((D, tt), lambda i, j: (0, i)),
        out_shape=jax.ShapeDtypeStruct((D, T), jnp.float32),
        scratch_shapes=[pltpu.VMEM((nb, tt), jnp.float32), pltpu.VMEM((nb, tt), jnp.bfloat16)],
        compiler_params=pltpu.CompilerParams(
            dimension_semantics=("parallel", "arbitrary"),
            vmem_limit_bytes=VMEM_LIMIT_BYTES),
    )(v, expert_u, expert_vt, s1a, e1a, s2, e2, thr)


def peer(u, w_q, sub_keys, expert_u, expert_v, tk=256, tt=512, nb=1024):
    ub = u.astype(jnp.bfloat16)
    q = pmm(ub, w_q.astype(jnp.bfloat16), out_dtype=jnp.bfloat16)
    s1, s2, e1, e2, thr = peer_topk(q, sub_keys, tk=tk)
    out_t = peer_experts(ub, expert_u.astype(jnp.bfloat16), expert_v.astype(jnp.bfloat16).T,
                         s1.transpose(1, 0, 2), e1.transpose(1, 0, 2), s2, e2, thr, tt=tt, nb=nb)
    return out_t.T


ROW_BLK = 256
STATE_MIN = -1e30

COL_QK, COL_V, COL_O = 0, 2 * ML_QK_W, 2 * ML_QK_W + ML_V_W
COL_HY = COL_O + ML_V_W
COL_BG = COL_HY + 3 * HY_W
COL_GATE = COL_BG + 2 * D_MODEL
IN_W_PAD = 16896


def _params(*sem):
    return pltpu.CompilerParams(dimension_semantics=sem, vmem_limit_bytes=VMEM_LIMIT_BYTES)


def _norm_mod_kernel(x_ref, c_ref, g_ref, sc_ref, sh_ref, o_ref, *, n_lat_blk):
    def emit(x):
        y = x * lax.rsqrt(jnp.mean(x * x, axis=-1, keepdims=True) + EPS)
        o_ref[...] = (y * g_ref[...] * (1.0 + sc_ref[0]) + sh_ref[0]).astype(o_ref.dtype)

    @pl.when(pl.program_id(0) < n_lat_blk)
    def _():
        emit(x_ref[...])

    @pl.when(pl.program_id(0) >= n_lat_blk)
    def _():
        emit(c_ref[...])


def norm_mod(x, ctx, g, sc, sh):
    L, D = x.shape
    n_lat_blk = L // ROW_BLK
    mod = pl.BlockSpec((1, 1, D), lambda i: (i // n_lat_blk, 0, 0))
    return pl.pallas_call(
        functools.partial(_norm_mod_kernel, n_lat_blk=n_lat_blk),
        grid=(n_lat_blk + 1,),
        in_specs=[pl.BlockSpec((ROW_BLK, D), lambda i: (jnp.minimum(i, n_lat_blk - 1), 0)),
                  pl.BlockSpec((ROW_BLK, D), lambda i: (0, 0)),
                  pl.BlockSpec((1, D), lambda i: (0, 0)), mod, mod],
        out_specs=pl.BlockSpec((ROW_BLK, D), lambda i: (i, 0)),
        out_shape=jax.ShapeDtypeStruct((L + ROW_BLK, D), jnp.bfloat16),
        compiler_params=_params("parallel"),
    )(x, ctx, g[None], sc, sh)


def _fill_halo(buf_ref, prev_ref, cur_ref, next_ref):
    r = cur_ref.shape[0]
    buf_ref[0:r, :] = prev_ref[...]
    buf_ref[r:2 * r, :] = cur_ref[...]
    buf_ref[2 * r:3 * r, :] = next_ref[...]


def _qk_conv_kernel(prev_ref, cur_ref, next_ref, w_ref, o_ref, buf_ref, *, n_lat_blk, q_slabs):
    j = pl.program_id(0)
    i = pl.program_id(1)
    r, cw = cur_ref.shape
    _fill_halo(buf_ref, prev_ref, cur_ref, next_ref)
    local = lax.broadcasted_iota(jnp.int32, (r, cw), 0)
    n_lat = n_lat_blk * r

    def tap(off):
        return buf_ref[pl.ds(r + off, r), :]

    def finish(acc):
        y = acc * jax.nn.sigmoid(acc)
        scale = jnp.where(j < q_slabs, ML_DQK ** -0.5, 1.0)
        o_ref[...] = (y * scale).astype(o_ref.dtype)

    @pl.when(i < n_lat_blk)
    def _():
        g = i * r + local
        col = jnp.bitwise_and(g, GRID_W - 1)
        acc = jnp.zeros((r, cw), jnp.float32)
        for dr in (-1, 0, 1):
            for dc in (-1, 0, 1):
                t = tap(GRID_W * dr + dc)
                if dr == -1:
                    t = jnp.where(g >= GRID_W, t, 0.0)
                if dr == 1:
                    t = jnp.where(g < n_lat - GRID_W, t, 0.0)
                if dc == -1:
                    t = jnp.where(col >= 1, t, 0.0)
                if dc == 1:
                    t = jnp.where(col < GRID_W - 1, t, 0.0)
                acc = acc + t * w_ref[3 * (dr + 1) + dc + 1:3 * (dr + 1) + dc + 2, :]
        finish(acc)

    @pl.when(i >= n_lat_blk)
    def _():
        acc = tap(0) * w_ref[4:5, :]
        acc = acc + jnp.where(local >= 1, tap(-1), 0.0) * w_ref[3:4, :]
        acc = acc + jnp.where(local < r - 1, tap(1), 0.0) * w_ref[5:6, :]
        finish(acc)


def _halo_specs(cw, col0_blk, nblk, row_off=0):
    cur = lambda j, i: (i + row_off, col0_blk + j)
    prev = lambda j, i: (jnp.maximum(i + row_off - 1, 0), col0_blk + j)
    nxt = lambda j, i: (jnp.minimum(i + row_off + 1, nblk - 1), col0_blk + j)
    return [pl.BlockSpec((ROW_BLK, cw), prev), pl.BlockSpec((ROW_BLK, cw), cur), pl.BlockSpec((ROW_BLK, cw), nxt)]


def qk_conv(p, w9, n_lat_blk, cw=256):
    T = p.shape[0]
    nblk = T // ROW_BLK
    width = 2 * ML_QK_W
    return pl.pallas_call(
        functools.partial(_qk_conv_kernel, n_lat_blk=n_lat_blk, q_slabs=ML_QK_W // cw),
        grid=(width // cw, nblk),
        in_specs=_halo_specs(cw, COL_QK // cw, nblk) + [pl.BlockSpec((9, cw), lambda j, i: (0, j))],
        out_specs=pl.BlockSpec((ROW_BLK, cw), lambda j, i: (i, j)),
        out_shape=jax.ShapeDtypeStruct((T, width), jnp.bfloat16),
        scratch_shapes=[pltpu.VMEM((3 * ROW_BLK, cw), jnp.float32)],
        compiler_params=_params("parallel", "parallel"),
    )(p, p, p, w9)


def _seq_conv_kernel(prev_ref, cur_ref, next_ref, w_ref, o_ref, buf_ref, *, n_rows):
    i = pl.program_id(1)
    r, cw = cur_ref.shape
    _fill_halo(buf_ref, prev_ref, cur_ref, next_ref)
    g = i * r + lax.broadcasted_iota(jnp.int32, (r, cw), 0)
    acc = buf_ref[pl.ds(r, r), :] * w_ref[1:2, :]
    acc = acc + jnp.where(g >= 1, buf_ref[pl.ds(r - 1, r), :], 0.0) * w_ref[0:1, :]
    acc = acc + jnp.where(g < n_rows - 1, buf_ref[pl.ds(r + 1, r), :], 0.0) * w_ref[2:3, :]
    o_ref[...] = acc.astype(o_ref.dtype)


def hy_conv(p, w3, n_lat_blk, cw=512):
    nblk = p.shape[0] // ROW_BLK
    width = 3 * HY_W
    n_rows = n_lat_blk * ROW_BLK
    return pl.pallas_call(
        functools.partial(_seq_conv_kernel, n_rows=n_rows),
        grid=(width // cw, n_lat_blk),
        in_specs=_halo_specs(cw, COL_HY // cw, nblk) + [pl.BlockSpec((3, cw), lambda j, i: (0, j))],
        out_specs=pl.BlockSpec((ROW_BLK, cw), lambda j, i: (i, j)),
        out_shape=jax.ShapeDtypeStruct((n_rows, width), jnp.float32),
        scratch_shapes=[pltpu.VMEM((3 * ROW_BLK, cw), jnp.float32)],
        compiler_params=_params("parallel", "parallel"),
    )(p, p, p, w3)


def _mlstm_kernel(q_ref, k_ref, kt_ref, v_ref, g_ref, o_ref, c_ref, n_ref, m_ref):
    d = pl.program_id(0)
    step = pl.program_id(2)
    r = q_ref.shape[0]

    @pl.when(step == 0)
    def _():
        c_ref[...] = jnp.zeros_like(c_ref)
        n_ref[...] = jnp.zeros_like(n_ref)
        m_ref[...] = jnp.full_like(m_ref, STATE_MIN)

    ig = g_ref[0, 0, 0:1, :]
    fg = g_ref[0, 0, 1:2, :]
    t_i = lax.broadcasted_iota(jnp.int32, (r, r), 0)
    s_i = lax.broadcasted_iota(jnp.int32, (r, r), 1)
    sgn = jnp.where(d == 0, 1, -1)
    reads = (s_i - t_i) * sgn <= 0
    read_by = (t_i - s_i) * sgn <= 0
    eye = s_i == t_i
    f_col = jnp.sum(jnp.where(eye, fg, 0.0), axis=1, keepdims=True)
    cum_col = jnp.sum(jnp.where(reads, fg, 0.0), axis=1, keepdims=True)
    cum_row = jnp.sum(jnp.where(read_by, f_col, 0.0), axis=0, keepdims=True)
    total = jnp.sum(fg, axis=1, keepdims=True)
    m_old = m_ref[...]

    dmat = jnp.where(reads, (cum_col - cum_row) + ig, NEG_INF)
    inter = cum_col + m_old
    m_t = jnp.maximum(inter, jnp.max(dmat, axis=1, keepdims=True))
    w = jnp.exp(dmat - m_t)
    s_inter = jnp.exp(inter - m_t)
    q = q_ref[...]
    v = v_ref[...]
    qkw = jnp.dot(q, kt_ref[...], preferred_element_type=jnp.float32) * w
    qn = jnp.sum(q.astype(jnp.float32) * n_ref[...], axis=1, keepdims=True)
    den = jnp.sum(qkw, axis=1, keepdims=True) + s_inter * qn
    num = jnp.dot(qkw.astype(jnp.bfloat16), v.astype(jnp.bfloat16), preferred_element_type=jnp.float32) \
        + s_inter * jnp.dot(q, c_ref[...].astype(jnp.bfloat16), preferred_element_type=jnp.float32)
    o_ref[0] = num / jnp.maximum(jnp.abs(den), jnp.exp(-m_t))

    dec = (total - cum_row) + ig
    m_new = jnp.maximum(total + m_old, jnp.max(dec, axis=1, keepdims=True))
    wk_row = jnp.exp(dec - m_new)
    s_old = jnp.exp(total + m_old - m_new)
    wk_col = jnp.sum(jnp.where(eye, wk_row, 0.0), axis=1, keepdims=True)
    c_ref[...] = s_old * c_ref[...] + jnp.dot(kt_ref[...], (wk_col * v).astype(jnp.bfloat16),
                                              preferred_element_type=jnp.float32)
    n_ref[...] = s_old * n_ref[...] + jnp.sum(wk_col * k_ref[...].astype(jnp.float32), axis=0, keepdims=True)
    m_ref[...] = m_new


def mlstm_scan(qk, kt, p, gates, n_lat_blk):
    T = qk.shape[0]
    r = ROW_BLK

    def blk(d, s):
        lat = jnp.where(d == 0, s - 1, n_lat_blk - s)
        return jnp.where(s == 0, n_lat_blk, lat)

    return pl.pallas_call(
        _mlstm_kernel,
        grid=(2, ML_HEADS, n_lat_blk + 1),
        in_specs=[pl.BlockSpec((r, ML_DQK), lambda d, h, s: (blk(d, s), h)),
                  pl.BlockSpec((r, ML_DQK), lambda d, h, s: (blk(d, s), ML_HEADS + h)),
                  pl.BlockSpec((ML_DQK, r), lambda d, h, s: (h, blk(d, s))),
                  pl.BlockSpec((r, ML_DV), lambda d, h, s: (blk(d, s), COL_V // ML_DV + h)),
                  pl.BlockSpec((1, 1, 2, r), lambda d, h, s: (d, h, 0, blk(d, s)))],
        out_specs=pl.BlockSpec((1, r, ML_DV), lambda d, h, s: (d, blk(d, s), h)),
        out_shape=jax.ShapeDtypeStruct((2, T, ML_V_W), jnp.float32),
        scratch_shapes=[pltpu.VMEM((ML_DQK, ML_DV), jnp.float32), pltpu.VMEM((1, ML_DQK), jnp.float32),
                        pltpu.VMEM((1, 1), jnp.float32)],
        compiler_params=_params("parallel", "parallel", "arbitrary"),
    )(qk, qk, kt, p, gates)


def _ml_out_kernel(hf_ref, hb_ref, o_ref, g_ref, out_ref):
    for h in range(ML_HEADS):
        ch = slice(h * ML_DV, (h + 1) * ML_DV)
        hh = hf_ref[0, :, ch] + hb_ref[0, :, ch]
        hh = hh * lax.rsqrt(jnp.mean(hh * hh, axis=-1, keepdims=True) + EPS)
        out_ref[:, ch] = (hh * g_ref[:, ch] * jax.nn.sigmoid(o_ref[:, ch])).astype(out_ref.dtype)


def ml_out(h2, p, gain, n_lat_blk):
    r = ROW_BLK
    return pl.pallas_call(
        _ml_out_kernel,
        grid=(n_lat_blk,),
        in_specs=[pl.BlockSpec((1, r, ML_V_W), lambda i: (0, i, 0)),
                  pl.BlockSpec((1, r, ML_V_W), lambda i: (1, i, 0)),
                  pl.BlockSpec((r, ML_V_W), lambda i: (i, COL_O // ML_V_W)),
                  pl.BlockSpec((1, ML_V_W), lambda i: (0, 0))],
        out_specs=pl.BlockSpec((r, ML_V_W), lambda i: (i, 0)),
        out_shape=jax.ShapeDtypeStruct((n_lat_blk * r, ML_V_W), jnp.bfloat16),
        compiler_params=_params("parallel"),
    )(h2, h2, p, gain[None])


def _merge_kernel(hm_ref, hh_ref, wm_ref, wh_ref, gm_ref, gh_ref, o_ref):
    ym = jnp.dot(hm_ref[...], wm_ref[...], preferred_element_type=jnp.float32)
    yh = jnp.dot(hh_ref[...], wh_ref[...], preferred_element_type=jnp.float32)
    o_ref[...] = (jax.nn.sigmoid(gm_ref[...]) * ym + jax.nn.sigmoid(gh_ref[...]) * yh).astype(o_ref.dtype)


def merge_branches(h_ml, h_hy, w_pm, w_ph, p, tm=512, tn=1024):
    L, K = h_ml.shape
    D = w_pm.shape[1]
    act = pl.BlockSpec((tm, K), lambda j, i: (i, 0))
    wgt = pl.BlockSpec((K, tn), lambda j, i: (0, j))
    return pl.pallas_call(
        _merge_kernel,
        grid=(D // tn, L // tm),
        in_specs=[act, act, wgt, wgt,
                  pl.BlockSpec((tm, tn), lambda j, i: (i, COL_BG // tn + j)),
                  pl.BlockSpec((tm, tn), lambda j, i: (i, (COL_BG + D) // tn + j))],
        out_specs=pl.BlockSpec((tm, tn), lambda j, i: (i, j)),
        out_shape=jax.ShapeDtypeStruct((L, D), jnp.bfloat16),
        compiler_params=_params("parallel", "parallel"),
    )(h_ml, h_hy, w_pm, w_ph, p, p)


def _out_proj_kernel(y_ref, w_ref, x_ref, g1_ref, ng_ref, sc_ref, sh_ref, h_ref, v_ref):
    h = x_ref[...] + g1_ref[...] * jnp.dot(y_ref[...], w_ref[...], preferred_element_type=jnp.float32)
    h_ref[...] = h
    n = h * lax.rsqrt(jnp.mean(h * h, axis=-1, keepdims=True) + EPS)
    v_ref[...] = (n * ng_ref[...] * (1.0 + sc_ref[...]) + sh_ref[...]).astype(v_ref.dtype)


def out_proj(y, w_o, x, g1, ng, sc, sh, tm=256):
    L, D = x.shape
    row = pl.BlockSpec((tm, D), lambda i: (i, 0))
    vec = pl.BlockSpec((1, D), lambda i: (0, 0))
    return pl.pallas_call(
        _out_proj_kernel,
        grid=(L // tm,),
        in_specs=[row, pl.BlockSpec((D, D), lambda i: (0, 0)), row, vec, vec, vec, vec],
        out_specs=[row, row],
        out_shape=[jax.ShapeDtypeStruct((L, D), jnp.float32), jax.ShapeDtypeStruct((L, D), jnp.bfloat16)],
        compiler_params=_params("parallel"),
    )(y, w_o, x, g1, ng, sc, sh)


def _final_kernel(h_ref, p_ref, g2_ref, fg_ref, o_ref):
    h = h_ref[...] + g2_ref[...] * p_ref[...]
    o_ref[...] = h * lax.rsqrt(jnp.mean(h * h, axis=-1, keepdims=True) + EPS) * fg_ref[...]


def final_norm(h, peer_out, g2, fg, tm=256):
    L, D = h.shape
    row = pl.BlockSpec((tm, D), lambda i: (i, 0))
    vec = pl.BlockSpec((1, D), lambda i: (0, 0))
    return pl.pallas_call(
        _final_kernel,
        grid=(L // tm,),
        in_specs=[row, row, vec, vec],
        out_specs=row,
        out_shape=jax.ShapeDtypeStruct((L, D), jnp.float32),
        compiler_params=_params("parallel"),
    )(h, peer_out, g2, fg)


def token_mixer_inputs(x, ctx, mods, norm_g, w_in):
    L = x.shape[0]
    assert ctx.shape[0] == ROW_BLK and L % ROW_BLK == 0
    n_lat_blk = L // ROW_BLK
    sh1, sc1 = mods[:, 0], mods[:, 1]
    u = norm_mod(x, ctx, norm_g, sc1[:, None, :], sh1[:, None, :])
    n_in = w_in.shape[1]
    w_r = jnp.concatenate([w_in[:, :COL_O + ML_V_W], w_in[:, COL_O + ML_V_W + 4 * ML_HEADS:],
                           w_in[:, COL_O + ML_V_W:COL_O + ML_V_W + 4 * ML_HEADS],
                           jnp.zeros((w_in.shape[0], IN_W_PAD - n_in), w_in.dtype)], axis=1).astype(jnp.bfloat16)
    return pmm(u, w_r, tm=768, tn=1536), n_lat_blk


def kernel(x, c, ctx, c_ctx, w_mod, b_mod, norm1_g, norm2_g, final_g, w_in, ml_conv_w, ml_gate_b, ml_norm_g, hy_conv_w, hy_w1, hy_b1, hy_w2, hy_b2, hy_w3, hy_freq, hy_bias, w_proj_ml, w_proj_hy, w_out, peer_wq, peer_keys, peer_u, peer_v):
    l = 0
    L, D = x.shape[1:]
    bf = jnp.bfloat16
    s_rows = jnp.zeros((8, D), jnp.float32).at[0].set(jax.nn.silu(c[0])).at[1].set(jax.nn.silu(c_ctx))
    mods = (pmm(s_rows, w_mod[l], tm=8, tn=1024)[:2] + b_mod[l]).reshape(2, 6, D)

    p, n_lat_blk = token_mixer_inputs(x[0], ctx[0], mods, norm1_g[l], w_in[l])
    T = p.shape[0]

    qk = qk_conv(p, ml_conv_w[l].reshape(9, 2 * ML_QK_W), n_lat_blk)
    kt = qk[:, ML_QK_W:].T
    gt = p[:, COL_GATE:COL_GATE + 4 * ML_HEADS].reshape(T, 4, ML_HEADS) + ml_gate_b[l]
    gates = jnp.stack([jnp.stack([gt[:, 0], jax.nn.log_sigmoid(gt[:, 1])]),
                       jnp.stack([gt[:, 2], jax.nn.log_sigmoid(gt[:, 3])])]).transpose(0, 3, 1, 2)
    h2 = mlstm_scan(qk, kt, p, gates, n_lat_blk)
    h_ml = ml_out(h2, p, ml_norm_g[l], n_lat_blk)

    xs = hy_conv(p, hy_conv_w[l], n_lat_blk)
    taps = hyena_filter_taps(L, hy_w1[l], hy_b1[l], hy_w2[l], hy_b2[l], hy_w3[l], hy_freq[l])
    h_hy = hyena(xs, taps, hy_bias[l])

    y = merge_branches(h_ml, h_hy.astype(bf), w_proj_ml[l].astype(bf), w_proj_hy[l].astype(bf), p)
    h_lat, v_lat = out_proj(y, w_out[l].astype(bf), x[0], mods[0:1, 2], norm2_g[l][None], mods[0:1, 4], mods[0:1, 3])
    peer_out = peer(v_lat, peer_wq[l], peer_keys[l], peer_u[l], peer_v[l])
    return final_norm(h_lat, peer_out, mods[0:1, 5], final_g[None])[None]


def _old_kernel(x, c, ctx, c_ctx, w_mod, b_mod, norm1_g, norm2_g, final_g, w_in, ml_conv_w, ml_gate_b, ml_norm_g, hy_conv_w, hy_w1, hy_b1, hy_w2, hy_b2, hy_w3, hy_freq, hy_bias, w_proj_ml, w_proj_hy, w_out, peer_wq, peer_keys, peer_u, peer_v):
    B, L, D = x.shape
    rows = L // GRID_W
    l = 0
    s_lat = jax.nn.silu(c)[:, None, :]
    s_ctx = jax.nn.silu(c_ctx)
    h_lat, h_ctx = x, ctx
    sh1, sc1, g1, sh2, sc2, g2 = jnp.split(s_lat @ w_mod[l] + b_mod[l], 6, axis=-1)
    csh1, csc1, cg1, csh2, csc2, cg2 = jnp.split(s_ctx @ w_mod[l] + b_mod[l], 6, axis=-1)

    u_lat = rmsnorm(h_lat, norm1_g[l]) * (1.0 + sc1) + sh1
    u_ctx = rmsnorm(h_ctx, norm1_g[l]) * (1.0 + csc1) + csh1
    w_in_p = jnp.pad(w_in[l], ((0, 0), (0, 16896 - w_in.shape[-1])))
    p_lat = pmm(u_lat[0], w_in_p)[None, :, :w_in.shape[-1]]
    p_ctx = pmm(u_ctx[0], w_in_p)[None, :, :w_in.shape[-1]]
    qk_l, v_l, o_l, gt_l, hy_l, bg_l = jnp.split(p_lat, IN_SPLIT, axis=-1)
    qk_c, v_c, o_c, gt_c, hy_c, bg_c = jnp.split(p_ctx, IN_SPLIT, axis=-1)

    qk_l = jax.nn.silu(dwconv_grid(qk_l, ml_conv_w[l], rows))
    qk_c = jax.nn.silu(dwconv1d(qk_c, ml_conv_w[l][1]))
    q_l, k_l, vh_l, gf_l, gb_l = mlstm_heads(qk_l, v_l, gt_l, ml_gate_b[l])
    q_c, k_c, vh_c, gf_c, gb_c = mlstm_heads(qk_c, v_c, gt_c, ml_gate_b[l])
    st_f = mlstm_final_state(k_c, vh_c, gf_c[0], gf_c[1])
    st_b = mlstm_final_state(_flip(k_c), _flip(vh_c), _flip(gb_c[0]), _flip(gb_c[1]))
    h_ml_l = mlstm_out(mlstm_bidir(q_l, k_l, vh_l, gf_l, gb_l, st_f, st_b), o_l, ml_norm_g[l])

    taps = hyena_filter_taps(L, hy_w1[l], hy_b1[l], hy_w2[l], hy_b2[l], hy_w3[l], hy_freq[l])
    h_hy_l = hyena(dwconv1d(hy_l, hy_conv_w[l])[0], taps, hy_bias[l])[None]
    mix_l = merge(h_ml_l, h_hy_l, bg_l, w_proj_ml[l], w_proj_hy[l], w_out[l])

    h_lat = h_lat + g1 * mix_l
    v_lat = rmsnorm(h_lat, norm2_g[l]) * (1.0 + sc2) + sh2
    h_lat = h_lat + g2 * peer(v_lat[0], peer_wq[l], peer_keys[l], peer_u[l], peer_v[l])[None]
    return rmsnorm(h_lat, final_g)
```

```python
import functools
import math
import numpy as np
import jax
import jax.numpy as jnp
from jax import lax
from jax.experimental import pallas as pl
from jax.experimental.pallas import tpu as pltpu

D_MODEL = 2048
GRID_W = 64
EPS = 1e-6

ML_HEADS = 8
ML_DQK = D_MODEL // 16
ML_DV = D_MODEL // 8
ML_QK_W = ML_HEADS * ML_DQK
ML_V_W = ML_HEADS * ML_DV

HY_W = D_MODEL
HY_ORDER = 2
HY_BANDS = 8
HY_SHIFT = 0.05
HY_MIN_DECAY = math.log(1e-2) / 1.5
HY_MAX_DECAY = math.log(1e-2) / 0.3

PEER_HEADS = 8
PEER_NKEYS = 128
PEER_DKEY = 256
PEER_TOPK = 16

LANES = 128
NEG_INF = float('-inf')
VMEM_LIMIT_BYTES = 56 * 1024 * 1024


def _mm_kernel(a_ref, b_ref, o_ref):
    o_ref[...] = jnp.dot(a_ref[...].astype(jnp.bfloat16), b_ref[...].astype(jnp.bfloat16),
                         preferred_element_type=jnp.float32).astype(o_ref.dtype)


def pmm(a, b, tm=512, tn=512, out_dtype=jnp.float32):
    M, K = a.shape
    _, N = b.shape
    tm = min(tm, M)
    tn = min(tn, N)
    assert M % tm == 0 and N % tn == 0, (M, N, tm, tn)
    return pl.pallas_call(
        _mm_kernel,
        grid=(N // tn, M // tm),
        in_specs=[pl.BlockSpec((tm, K), lambda j, i: (i, 0)),
                  pl.BlockSpec((K, tn), lambda j, i: (0, j))],
        out_specs=pl.BlockSpec((tm, tn), lambda j, i: (i, j)),
        out_shape=jax.ShapeDtypeStruct((M, N), out_dtype),
        compiler_params=pltpu.CompilerParams(
            dimension_semantics=("parallel", "parallel"),
            vmem_limit_bytes=VMEM_LIMIT_BYTES),
    )(a, b)


FFT_R = 128
FFT_N = FFT_R * FFT_R
HY_FEAT_PAD = 32
HIGHEST = lax.Precision.HIGHEST


def _dft_tables():
    r = np.arange(FFT_R)
    ang = 2.0 * np.pi * np.outer(r, r) / FFT_R
    c, s = np.cos(ang), np.sin(ang)
    first = np.concatenate([c, -s], axis=0)
    mid_fwd = np.block([[c, s], [-s, c]])
    mid_inv = np.block([[c, -s], [s, c]])
    last = np.concatenate([c, -s], axis=1) / FFT_N
    tw = 2.0 * np.pi * np.outer(r, r) / FFT_N
    f32 = lambda a: jnp.asarray(a, jnp.float32)
    return f32(first), f32(mid_fwd), f32(mid_inv), f32(last), f32(np.cos(tw)), f32(np.sin(tw))


def _fft_mid_kernel(a_ref, tc_ref, ts_ref, mf_ref, mi_ref, *rest, inverse):
    if inverse:
        g_ref, o_ref, b_ref = rest
    else:
        o_ref, b_ref = rest
    cb = a_ref.shape[-1]
    tc = tc_ref[0]
    ts = ts_ref[0]
    for lt in range(cb // LANES):
        ch = slice(lt * LANES, (lt + 1) * LANES)
        ar = a_ref[0, 0, :, ch]
        ai = a_ref[1, 0, :, ch]
        b_ref[0:FFT_R, ch] = (ar * tc + ai * ts).astype(b_ref.dtype)
        b_ref[FFT_R:, ch] = (ai * tc - ar * ts).astype(b_ref.dtype)
    x = jnp.dot(mf_ref[...], b_ref[...], preferred_element_type=jnp.float32)
    if not inverse:
        o_ref[0, 0] = x[:FFT_R].astype(o_ref.dtype)
        o_ref[1, 0] = x[FFT_R:].astype(o_ref.dtype)
        return
    xr, xi = x[:FFT_R], x[FFT_R:]
    gr = g_ref[0, 0].astype(jnp.float32)
    gi = g_ref[1, 0].astype(jnp.float32)
    b_ref[0:FFT_R, :] = (xr * gr - xi * gi).astype(b_ref.dtype)
    b_ref[FFT_R:, :] = (xr * gi + xi * gr).astype(b_ref.dtype)
    q = jnp.dot(mi_ref[...], b_ref[...], preferred_element_type=jnp.float32)
    for lt in range(cb // LANES):
        ch = slice(lt * LANES, (lt + 1) * LANES)
        qr = q[:FFT_R, ch]
        qi = q[FFT_R:, ch]
        o_ref[0, 0, :, ch] = (qr * tc - qi * ts).astype(o_ref.dtype)
        o_ref[1, 0, :, ch] = (qi * tc + qr * ts).astype(o_ref.dtype)


def fft_mid(a, tc, ts, mid_fwd, mid_inv, g=None, cb=2048, out_dtype=jnp.float32):
    C = a.shape[-1]
    cb = min(cb, C)
    inverse = g is not None
    blk = pl.BlockSpec((2, 1, FFT_R, cb), lambda k, j: (0, k, 0, j))
    tw = pl.BlockSpec((1, FFT_R, LANES), lambda k, j: (k, 0, 0))
    mat = pl.BlockSpec((2 * FFT_R, 2 * FFT_R), lambda k, j: (0, 0))
    return pl.pallas_call(
        functools.partial(_fft_mid_kernel, inverse=inverse),
        grid=(FFT_R, C // cb),
        in_specs=[blk, tw, tw, mat, mat] + ([blk] if inverse else []),
        out_specs=blk,
        out_shape=jax.ShapeDtypeStruct(a.shape, out_dtype),
        scratch_shapes=[pltpu.VMEM((2 * FFT_R, cb), jnp.bfloat16)],
        compiler_params=pltpu.CompilerParams(
            dimension_semantics=("parallel", "parallel"),
            vmem_limit_bytes=VMEM_LIMIT_BYTES),
    )(a, tc, ts, mid_fwd.astype(jnp.bfloat16), mid_inv.astype(jnp.bfloat16), *([g] if inverse else []))


FFT_GRP = 32


def _fft_first_kernel(x_ref, f_ref, o_ref):
    rows, cb = x_ref.shape
    n1 = rows // FFT_R
    kk = f_ref.shape[0]
    fb = jnp.broadcast_to(f_ref[...][None], (FFT_GRP, kk, n1))
    for g in range(FFT_R // FFT_GRP):
        n2 = slice(g * FFT_GRP, (g + 1) * FFT_GRP)
        xt = jnp.swapaxes(x_ref[...].reshape(n1, FFT_R, cb)[:, n2, :], 0, 1).astype(jnp.bfloat16)
        a = lax.dot_general(fb, xt, (((2,), (1,)), ((0,), (0,))), preferred_element_type=jnp.float32)
        o_ref[:, n2, :] = jnp.swapaxes(a, 0, 1).astype(o_ref.dtype)


def fft_first(x, rows, row_blk, col0, width, first, cb=128):
    n1 = rows // FFT_R
    return pl.pallas_call(
        _fft_first_kernel,
        grid=(width // cb,),
        in_specs=[pl.BlockSpec((rows, cb), lambda j: (row_blk, col0 // cb + j)),
                  pl.BlockSpec((2 * FFT_R, n1), lambda j: (0, 0))],
        out_specs=pl.BlockSpec((2 * FFT_R, FFT_R, cb), lambda j: (0, 0, j)),
        out_shape=jax.ShapeDtypeStruct((2 * FFT_R, FFT_R, width), jnp.bfloat16),
        compiler_params=pltpu.CompilerParams(
            dimension_semantics=("parallel",), vmem_limit_bytes=VMEM_LIMIT_BYTES),
    )(x, first[:, :n1].astype(jnp.bfloat16))


def _fft_last_kernel(q_ref, f_ref, z_ref, x_ref, b_ref, o_ref):
    kk, _, cb = q_ref.shape
    n1 = f_ref.shape[0]
    fb = jnp.broadcast_to(f_ref[...][None], (FFT_GRP, n1, kk))
    for g in range(FFT_R // FFT_GRP):
        qt = jnp.swapaxes(q_ref[:, g * FFT_GRP:(g + 1) * FFT_GRP, :].astype(jnp.float32), 0, 1)
        y = lax.dot_general(fb, qt.astype(jnp.bfloat16), (((2,), (1,)), ((0,), (0,))),
                            preferred_element_type=jnp.float32)
        yt = jnp.swapaxes(y, 0, 1)
        for i in range(n1):
            rows = slice(i * FFT_R + g * FFT_GRP, i * FFT_R + (g + 1) * FFT_GRP)
            o_ref[rows, :] = (x_ref[rows, :] * (yt[i] + z_ref[rows, :] * b_ref[...])).astype(o_ref.dtype)


def fft_last(q, last, z, z_col0, xg, xg_col0, bias, out_dtype, cb=128):
    C = q.shape[-1]
    L = z.shape[0]
    n1 = L // FFT_R
    return pl.pallas_call(
        _fft_last_kernel,
        grid=(C // cb,),
        in_specs=[pl.BlockSpec((2 * FFT_R, FFT_R, cb), lambda j: (0, 0, j)),
                  pl.BlockSpec((n1, 2 * FFT_R), lambda j: (0, 0)),
                  pl.BlockSpec((L, cb), lambda j: (0, z_col0 // cb + j)),
                  pl.BlockSpec((L, cb), lambda j: (0, xg_col0 // cb + j)),
                  pl.BlockSpec((1, cb), lambda j: (0, j))],
        out_specs=pl.BlockSpec((L, cb), lambda j: (0, j)),
        out_shape=jax.ShapeDtypeStruct((L, C), out_dtype),
        compiler_params=pltpu.CompilerParams(
            dimension_semantics=("parallel",), vmem_limit_bytes=VMEM_LIMIT_BYTES),
    )(q, last[:n1].astype(jnp.bfloat16), z, xg, bias)


def _hyena_filter_kernel(ft_ref, tn_ref, w1_ref, b1_ref, w2_ref, b2_ref, fr_ref, w3_ref, dl_ref, o_ref,
                         hdn_ref, win_ref):
    cols = o_ref.shape[-1]

    @pl.when(pl.program_id(1) == 0)
    def _():
        pre = jnp.dot(ft_ref[...], w1_ref[...], preferred_element_type=jnp.float32, precision=HIGHEST)
        hdn = jnp.sin(fr_ref[...] * (pre + b1_ref[...]))
        pre = jnp.dot(hdn, w2_ref[...], preferred_element_type=jnp.float32, precision=HIGHEST)
        hdn_ref[...] = jnp.sin(fr_ref[...] * (pre + b2_ref[...])).astype(hdn_ref.dtype)
        tn = tn_ref[...]
        for lt in range(cols // LANES):
            ch = slice(lt * LANES, (lt + 1) * LANES)
            win_ref[:, ch] = jnp.exp(-tn * dl_ref[:, ch]) + HY_SHIFT

    filt = jnp.dot(hdn_ref[...], w3_ref[0].astype(jnp.bfloat16), preferred_element_type=jnp.float32)
    o_ref[0] = filt * win_ref[...]


def hyena_filter_taps(L, w1, b1, w2, b2, w3, freq, tr=512):
    assert FFT_N == 2 * L
    n = np.arange(FFT_N)
    t = np.where(n < L, n, (FFT_N - n) % L).astype(np.float32)
    tnorm = t / np.float32(L)
    bands = np.linspace(1e-4, HY_BANDS - 1, HY_BANDS, dtype=np.float32)
    ang = (np.float32(2.0 * math.pi / L) * t[:, None] * bands[None, :]).astype(np.float64)
    feats = np.concatenate([tnorm[:, None], np.cos(ang), -np.sin(ang)], axis=-1)
    feats = np.pad(feats, ((0, 0), (0, HY_FEAT_PAD - feats.shape[1]))).astype(np.float32)
    tmark = np.broadcast_to(tnorm[:, None], (FFT_N, LANES))
    deltas = np.abs(np.linspace(HY_MIN_DECAY, HY_MAX_DECAY, HY_W, dtype=np.float32))[None, :]
    w1p = jnp.pad(w1, ((0, HY_FEAT_PAD - w1.shape[0]), (0, 0)))
    ffn = w2.shape[0]
    w3r = w3.reshape(ffn, 2 * HY_ORDER, HY_W).transpose(1, 0, 2)
    half = L // tr
    row = lambda r, o: (r, 0)
    fix = lambda r, o: (0, 0)
    return pl.pallas_call(
        _hyena_filter_kernel,
        grid=(FFT_N // tr, HY_ORDER),
        in_specs=[pl.BlockSpec((tr, HY_FEAT_PAD), row),
                  pl.BlockSpec((tr, LANES), row),
                  pl.BlockSpec((HY_FEAT_PAD, ffn), fix),
                  pl.BlockSpec((1, ffn), fix),
                  pl.BlockSpec((ffn, ffn), fix),
                  pl.BlockSpec((1, ffn), fix),
                  pl.BlockSpec((1, ffn), fix),
                  pl.BlockSpec((1, ffn, HY_W), lambda r, o: ((r // half) * HY_ORDER + o, 0, 0)),
                  pl.BlockSpec((1, HY_W), fix)],
        out_specs=pl.BlockSpec((1, tr, HY_W), lambda r, o: (o, r, 0)),
        out_shape=jax.ShapeDtypeStruct((HY_ORDER, FFT_N, HY_W), jnp.float32),
        scratch_shapes=[pltpu.VMEM((tr, ffn), jnp.bfloat16), pltpu.VMEM((tr, HY_W), jnp.float32)],
        compiler_params=pltpu.CompilerParams(
            dimension_semantics=("parallel", "arbitrary"), vmem_limit_bytes=VMEM_LIMIT_BYTES),
    )(jnp.asarray(feats), jnp.asarray(tmark), w1p, b1[None], w2, b2[None], freq[None], w3r, jnp.asarray(deltas))


def hyena(xs, taps, bias):
    L = xs.shape[0]
    C = xs.shape[1] // 3
    first, mid_fwd, mid_inv, last, tcos, tsin = _dft_tables()
    tc = jnp.broadcast_to(tcos[:, :, None], (FFT_R, FFT_R, LANES))
    ts = jnp.broadcast_to(tsin[:, :, None], (FFT_R, FFT_R, LANES))
    shape4 = (2, FFT_R, FFT_R, C)
    bf = jnp.bfloat16
    taps2 = taps.reshape(HY_ORDER * FFT_N, C)
    z, z_col0 = xs, 2 * C
    for o in range(HY_ORDER):
        ga = fft_first(taps2, FFT_N, o, 0, C, first)
        g = fft_mid(ga.reshape(shape4), tc, ts, mid_fwd, mid_inv, out_dtype=bf)
        a = fft_first(z, L, 0, z_col0, C, first)
        q = fft_mid(a.reshape(shape4), tc, ts, mid_fwd, mid_inv, g=g, out_dtype=bf)
        bias_eff = (bias[o] + taps[o, L])[None, :]
        z = fft_last(q.reshape(2 * FFT_R, FFT_R, C), last, z, z_col0, xs, o * C, bias_eff,
                     jnp.float32 if o + 1 < HY_ORDER else bf)
        z_col0 = 0
    return z


_CAND_ROWS = tuple(PEER_TOPK // (i + 1) for i in range(PEER_TOPK))


def _extract_top(s, rounds):
    rows = s.shape[0]
    iota = lax.broadcasted_iota(jnp.int32, s.shape, 0).astype(jnp.float32)
    tops = []
    for _ in range(rounds):
        m = jnp.max(s, axis=0, keepdims=True)
        first = jnp.min(jnp.where(s == m, iota, float(rows)), axis=0, keepdims=True)
        s = jnp.where(iota == first, NEG_INF, s)
        tops.append(m)
    return tops


def _peer_topk_kernel(q_ref, k_ref, s1_ref, s2_ref, e1_ref, e2_ref, thr_ref):
    tk = q_ref.shape[0]
    for u in range(tk // LANES):
        tok = slice(u * LANES, (u + 1) * LANES)
        s = []
        tops = []
        for p in range(2):
            qp = q_ref[tok, p * PEER_NKEYS:(p + 1) * PEER_NKEYS]
            sp = lax.dot_general(k_ref[0, p], qp, (((1,), (1,)), ((), ())),
                                 preferred_element_type=jnp.float32)
            s.append(sp)
            tops.append(_extract_top(sp, PEER_TOPK))
        top2a = jnp.concatenate(tops[1][:8], axis=0)
        top2b = jnp.concatenate(tops[1][8:], axis=0)
        row = lax.broadcasted_iota(jnp.int32, (8, LANES), 0)
        cands = [tops[0][0] + top2a, tops[0][0] + top2b]
        for i in range(1, PEER_TOPK):
            cands.append(jnp.where(row < _CAND_ROWS[i], tops[0][i] + top2a, NEG_INF))
        best = _extract_top(jnp.concatenate(cands, axis=0), PEER_TOPK)
        z = jnp.ones_like(best[0])
        for b in best[1:]:
            z = z + jnp.exp(b - best[0])
        s1_ref[0, :, tok] = s[0]
        s2_ref[0, :, tok] = s[1]
        e1_ref[0, :, tok] = jnp.exp(s[0] - tops[0][0]) / z
        e2_ref[0, :, tok] = jnp.exp(s[1] - tops[1][0])
        thr_ref[0, :, tok] = best[-1]


def peer_topk(q, sub_keys, tk=256):
    T = q.shape[0]
    H = PEER_HEADS
    big = jax.ShapeDtypeStruct((H, PEER_NKEYS, T), jnp.float32)
    big_spec = pl.BlockSpec((1, PEER_NKEYS, tk), lambda i, h: (h, 0, i))
    s1, s2, e1, e2, thr = pl.pallas_call(
        _peer_topk_kernel,
        grid=(T // tk, H),
        in_specs=[pl.BlockSpec((tk, PEER_DKEY), lambda i, h: (i, h)),
                  pl.BlockSpec((1, 2, PEER_NKEYS, PEER_DKEY // 2), lambda i, h: (h, 0, 0, 0))],
        out_specs=[big_spec, big_spec, big_spec, big_spec,
                   pl.BlockSpec((1, 1, tk), lambda i, h: (h, 0, i))],
        out_shape=[big, big, big, big, jax.ShapeDtypeStruct((H, 1, T), jnp.float32)],
        compiler_params=pltpu.CompilerParams(
            dimension_semantics=("parallel", "parallel"),
            vmem_limit_bytes=VMEM_LIMIT_BYTES),
    )(q, sub_keys.astype(jnp.bfloat16))
    return s1, s2, e1, e2, thr.reshape(H, T)


def _peer_expert_kernel(v_ref, u_ref, vt_ref, s1a_ref, e1a_ref, s2_ref, e2_ref, thr_ref, o_ref, sc_ref, w_ref):
    j = pl.program_id(1)
    tt = v_ref.shape[0]
    na = u_ref.shape[0] // PEER_NKEYS

    def scores():
        sc_ref[...] = lax.dot_general(u_ref[...], v_ref[...], (((1,), (1,)), ((), ())),
                                      preferred_element_type=jnp.float32)

    def weights(al, w_ref):
        rows = pl.ds(pl.multiple_of(al * PEER_NKEYS, PEER_NKEYS), PEER_NKEYS)
        for u in range(tt // LANES):
            tok = slice(u * LANES, (u + 1) * LANES)
            g = jnp.zeros((PEER_NKEYS, LANES), jnp.float32)
            for h in range(PEER_HEADS):
                s1row = s1a_ref[al, h:h + 1, tok]
                e1row = e1a_ref[al, h:h + 1, tok]
                keep = (s1row + s2_ref[h, :, tok]) >= thr_ref[h:h + 1, tok]
                g = g + jnp.where(keep, e1row * e2_ref[h, :, tok], 0.0)
            sc = sc_ref[rows, tok]
            act = 0.5 * sc * (1.0 + lax.erf(sc * math.sqrt(0.5)))
            w_ref[rows, tok] = (act * g).astype(w_ref.dtype)

    @pl.when(j == 0)
    def _():
        o_ref[...] = jnp.zeros_like(o_ref)

    scores()

    def body(al, carry):
        weights(al, w_ref)
        return carry
    lax.fori_loop(0, na, body, 0)
    o_ref[...] += jnp.dot(vt_ref[...], w_ref[...], preferred_element_type=jnp.float32)


def peer_experts(v, expert_u, expert_vt, s1a, e1a, s2, e2, thr, tt=512, nb=1024):
    T, D = v.shape
    N = expert_u.shape[0]
    na = nb // PEER_NKEYS
    H = PEER_HEADS
    return pl.pallas_call(
        _peer_expert_kernel,
        grid=(T // tt, N // nb),
        in_specs=[pl.BlockSpec((tt, D), lambda i, j: (i, 0)),
                  pl.BlockSpec((nb, D), lambda i, j: (j, 0)),
                  pl.BlockSpec((D, nb), lambda i, j: (0, j)),
                  pl.BlockSpec((na, H, tt), lambda i, j: (j, 0, i)),
                  pl.BlockSpec((na, H, tt), lambda i, j: (j, 0, i)),
                  pl.BlockSpec((H, PEER_NKEYS, tt), lambda i, j: (0, 0, i)),
                  pl.BlockSpec((H, PEER_NKEYS, tt), lambda i, j: (0, 0, i)),
                  pl.BlockSpec((H, tt), lambda i, j: (0, i))],
        out_specs=pl.BlockSpec((D, tt), lambda i, j: (0, i)),
        out_shape=jax.ShapeDtypeStruct((D, T), jnp.float32),
        scratch_shapes=[pltpu.VMEM((nb, tt), jnp.float32), pltpu.VMEM((nb, tt), jnp.bfloat16)],
        compiler_params=pltpu.CompilerParams(
            dimension_semantics=("parallel", "arbitrary"),
            vmem_limit_bytes=VMEM_LIMIT_BYTES),
    )(v, expert_u, expert_vt, s1a, e1a, s2, e2, thr)


def peer(u, w_q, sub_keys, expert_u, expert_v, tk=256, tt=512, nb=1024):
    ub = u.astype(jnp.bfloat16)
    q = pmm(ub, w_q.astype(jnp.bfloat16), out_dtype=jnp.bfloat16)
    s1, s2, e1, e2, thr = peer_topk(q, sub_keys, tk=tk)
    out_t = peer_experts(ub, expert_u.astype(jnp.bfloat16), expert_v.astype(jnp.bfloat16).T,
                         s1.transpose(1, 0, 2), e1.transpose(1, 0, 2), s2, e2, thr, tt=tt, nb=nb)
    return out_t.T


ROW_BLK = 256
STATE_MIN = -1e30

COL_QK, COL_V, COL_O = 0, 2 * ML_QK_W, 2 * ML_QK_W + ML_V_W
COL_HY = COL_O + ML_V_W
COL_BG = COL_HY + 3 * HY_W
COL_GATE = COL_BG + 2 * D_MODEL
IN_W_PAD = 16896


def _params(*sem):
    return pltpu.CompilerParams(dimension_semantics=sem, vmem_limit_bytes=VMEM_LIMIT_BYTES)


def _norm_mod_kernel(x_ref, c_ref, g_ref, sc_ref, sh_ref, o_ref, *, n_lat_blk):
    def emit(x):
        y = x * lax.rsqrt(jnp.mean(x * x, axis=-1, keepdims=True) + EPS)
        o_ref[...] = (y * g_ref[...] * (1.0 + sc_ref[0]) + sh_ref[0]).astype(o_ref.dtype)

    @pl.when(pl.program_id(0) < n_lat_blk)
    def _():
        emit(x_ref[...])

    @pl.when(pl.program_id(0) >= n_lat_blk)
    def _():
        emit(c_ref[...])


def norm_mod(x, ctx, g, sc, sh):
    L, D = x.shape
    n_lat_blk = L // ROW_BLK
    mod = pl.BlockSpec((1, 1, D), lambda i: (i // n_lat_blk, 0, 0))
    return pl.pallas_call(
        functools.partial(_norm_mod_kernel, n_lat_blk=n_lat_blk),
        grid=(n_lat_blk + 1,),
        in_specs=[pl.BlockSpec((ROW_BLK, D), lambda i: (jnp.minimum(i, n_lat_blk - 1), 0)),
                  pl.BlockSpec((ROW_BLK, D), lambda i: (0, 0)),
                  pl.BlockSpec((1, D), lambda i: (0, 0)), mod, mod],
        out_specs=pl.BlockSpec((ROW_BLK, D), lambda i: (i, 0)),
        out_shape=jax.ShapeDtypeStruct((L + ROW_BLK, D), jnp.bfloat16),
        compiler_params=_params("parallel"),
    )(x, ctx, g[None], sc, sh)


def _fill_halo(buf_ref, prev_ref, cur_ref, next_ref):
    r = cur_ref.shape[0]
    buf_ref[0:r, :] = prev_ref[...]
    buf_ref[r:2 * r, :] = cur_ref[...]
    buf_ref[2 * r:3 * r, :] = next_ref[...]


def _qk_conv_kernel(prev_ref, cur_ref, next_ref, w_ref, o_ref, buf_ref, *, n_lat_blk, q_slabs):
    j = pl.program_id(0)
    i = pl.program_id(1)
    r, cw = cur_ref.shape
    _fill_halo(buf_ref, prev_ref, cur_ref, next_ref)
    local = lax.broadcasted_iota(jnp.int32, (r, cw), 0)
    n_lat = n_lat_blk * r

    def tap(off):
        return buf_ref[pl.ds(r + off, r), :]

    def finish(acc):
        y = acc * jax.nn.sigmoid(acc)
        scale = jnp.where(j < q_slabs, ML_DQK ** -0.5, 1.0)
        o_ref[...] = (y * scale).astype(o_ref.dtype)

    @pl.when(i < n_lat_blk)
    def _():
        g = i * r + local
        col = jnp.bitwise_and(g, GRID_W - 1)
        acc = jnp.zeros((r, cw), jnp.float32)
        for dr in (-1, 0, 1):
            for dc in (-1, 0, 1):
                t = tap(GRID_W * dr + dc)
                if dr == -1:
                    t = jnp.where(g >= GRID_W, t, 0.0)
                if dr == 1:
                    t = jnp.where(g < n_lat - GRID_W, t, 0.0)
                if dc == -1:
                    t = jnp.where(col >= 1, t, 0.0)
                if dc == 1:
                    t = jnp.where(col < GRID_W - 1, t, 0.0)
                acc = acc + t * w_ref[3 * (dr + 1) + dc + 1:3 * (dr + 1) + dc + 2, :]
        finish(acc)

    @pl.when(i >= n_lat_blk)
    def _():
        acc = tap(0) * w_ref[4:5, :]
        acc = acc + jnp.where(local >= 1, tap(-1), 0.0) * w_ref[3:4, :]
        acc = acc + jnp.where(local < r - 1, tap(1), 0.0) * w_ref[5:6, :]
        finish(acc)


def _halo_specs(cw, col0_blk, nblk, row_off=0):
    cur = lambda j, i: (i + row_off, col0_blk + j)
    prev = lambda j, i: (jnp.maximum(i + row_off - 1, 0), col0_blk + j)
    nxt = lambda j, i: (jnp.minimum(i + row_off + 1, nblk - 1), col0_blk + j)
    return [pl.BlockSpec((ROW_BLK, cw), prev), pl.BlockSpec((ROW_BLK, cw), cur), pl.BlockSpec((ROW_BLK, cw), nxt)]


def qk_conv(p, w9, n_lat_blk, cw=256):
    T = p.shape[0]
    nblk = T // ROW_BLK
    width = 2 * ML_QK_W
    return pl.pallas_call(
        functools.partial(_qk_conv_kernel, n_lat_blk=n_lat_blk, q_slabs=ML_QK_W // cw),
        grid=(width // cw, nblk),
        in_specs=_halo_specs(cw, COL_QK // cw, nblk) + [pl.BlockSpec((9, cw), lambda j, i: (0, j))],
        out_specs=pl.BlockSpec((ROW_BLK, cw), lambda j, i: (i, j)),
        out_shape=jax.ShapeDtypeStruct((T, width), jnp.bfloat16),
        scratch_shapes=[pltpu.VMEM((3 * ROW_BLK, cw), jnp.float32)],
        compiler_params=_params("parallel", "parallel"),
    )(p, p, p, w9)


def _seq_conv_kernel(prev_ref, cur_ref, next_ref, w_ref, o_ref, buf_ref, *, n_rows):
    i = pl.program_id(1)
    r, cw = cur_ref.shape
    _fill_halo(buf_ref, prev_ref, cur_ref, next_ref)
    g = i * r + lax.broadcasted_iota(jnp.int32, (r, cw), 0)
    acc = buf_ref[pl.ds(r, r), :] * w_ref[1:2, :]
    acc = acc + jnp.where(g >= 1, buf_ref[pl.ds(r - 1, r), :], 0.0) * w_ref[0:1, :]
    acc = acc + jnp.where(g < n_rows - 1, buf_ref[pl.ds(r + 1, r), :], 0.0) * w_ref[2:3, :]
    o_ref[...] = acc.astype(o_ref.dtype)


def hy_conv(p, w3, n_lat_blk, cw=512):
    nblk = p.shape[0] // ROW_BLK
    width = 3 * HY_W
    n_rows = n_lat_blk * ROW_BLK
    return pl.pallas_call(
        functools.partial(_seq_conv_kernel, n_rows=n_rows),
        grid=(width // cw, n_lat_blk),
        in_specs=_halo_specs(cw, COL_HY // cw, nblk) + [pl.BlockSpec((3, cw), lambda j, i: (0, j))],
        out_specs=pl.BlockSpec((ROW_BLK, cw), lambda j, i: (i, j)),
        out_shape=jax.ShapeDtypeStruct((n_rows, width), jnp.float32),
        scratch_shapes=[pltpu.VMEM((3 * ROW_BLK, cw), jnp.float32)],
        compiler_params=_params("parallel", "parallel"),
    )(p, p, p, w3)


def _mlstm_kernel(q_ref, k_ref, kt_ref, v_ref, g_ref, o_ref, c_ref, n_ref, m_ref):
    d = pl.program_id(0)
    step = pl.program_id(2)
    r = q_ref.shape[0]

    @pl.when(step == 0)
    def _():
        c_ref[...] = jnp.zeros_like(c_ref)
        n_ref[...] = jnp.zeros_like(n_ref)
        m_ref[...] = jnp.full_like(m_ref, STATE_MIN)

    ig = g_ref[0, 0, 0:1, :]
    fg = g_ref[0, 0, 1:2, :]
    t_i = lax.broadcasted_iota(jnp.int32, (r, r), 0)
    s_i = lax.broadcasted_iota(jnp.int32, (r, r), 1)
    sgn = jnp.where(d == 0, 1, -1)
    reads = (s_i - t_i) * sgn <= 0
    read_by = (t_i - s_i) * sgn <= 0
    eye = s_i == t_i
    f_col = jnp.sum(jnp.where(eye, fg, 0.0), axis=1, keepdims=True)
    cum_col = jnp.sum(jnp.where(reads, fg, 0.0), axis=1, keepdims=True)
    cum_row = jnp.sum(jnp.where(read_by, f_col, 0.0), axis=0, keepdims=True)
    total = jnp.sum(fg, axis=1, keepdims=True)
    m_old = m_ref[...]

    dmat = jnp.where(reads, (cum_col - cum_row) + ig, NEG_INF)
    inter = cum_col + m_old
    m_t = jnp.maximum(inter, jnp.max(dmat, axis=1, keepdims=True))
    w = jnp.exp(dmat - m_t)
    s_inter = jnp.exp(inter - m_t)
    q = q_ref[...]
    v = v_ref[...]
    qkw = jnp.dot(q, kt_ref[...], preferred_element_type=jnp.float32) * w
    qn = jnp.sum(q.astype(jnp.float32) * n_ref[...], axis=1, keepdims=True)
    den = jnp.sum(qkw, axis=1, keepdims=True) + s_inter * qn
    num = jnp.dot(qkw.astype(jnp.bfloat16), v.astype(jnp.bfloat16), preferred_element_type=jnp.float32) \
        + s_inter * jnp.dot(q, c_ref[...].astype(jnp.bfloat16), preferred_element_type=jnp.float32)
    o_ref[0] = num / jnp.maximum(jnp.abs(den), jnp.exp(-m_t))

    dec = (total - cum_row) + ig
    m_new = jnp.maximum(total + m_old, jnp.max(dec, axis=1, keepdims=True))
    wk_row = jnp.exp(dec - m_new)
    s_old = jnp.exp(total + m_old - m_new)
    wk_col = jnp.sum(jnp.where(eye, wk_row, 0.0), axis=1, keepdims=True)
    c_ref[...] = s_old * c_ref[...] + jnp.dot(kt_ref[...], (wk_col * v).astype(jnp.bfloat16),
                                              preferred_element_type=jnp.float32)
    n_ref[...] = s_old * n_ref[...] + jnp.sum(wk_col * k_ref[...].astype(jnp.float32), axis=0, keepdims=True)
    m_ref[...] = m_new


def mlstm_scan(qk, kt, p, gates, n_lat_blk):
    T = qk.shape[0]
    r = ROW_BLK

    def blk(d, s):
        lat = jnp.where(d == 0, s - 1, n_lat_blk - s)
        return jnp.where(s == 0, n_lat_blk, lat)

    return pl.pallas_call(
        _mlstm_kernel,
        grid=(2, ML_HEADS, n_lat_blk + 1),
        in_specs=[pl.BlockSpec((r, ML_DQK), lambda d, h, s: (blk(d, s), h)),
                  pl.BlockSpec((r, ML_DQK), lambda d, h, s: (blk(d, s), ML_HEADS + h)),
                  pl.BlockSpec((ML_DQK, r), lambda d, h, s: (h, blk(d, s))),
                  pl.BlockSpec((r, ML_DV), lambda d, h, s: (blk(d, s), COL_V // ML_DV + h)),
                  pl.BlockSpec((1, 1, 2, r), lambda d, h, s: (d, h, 0, blk(d, s)))],
        out_specs=pl.BlockSpec((1, r, ML_DV), lambda d, h, s: (d, blk(d, s), h)),
        out_shape=jax.ShapeDtypeStruct((2, T, ML_V_W), jnp.float32),
        scratch_shapes=[pltpu.VMEM((ML_DQK, ML_DV), jnp.float32), pltpu.VMEM((1, ML_DQK), jnp.float32),
                        pltpu.VMEM((1, 1), jnp.float32)],
        compiler_params=_params("parallel", "parallel", "arbitrary"),
    )(qk, qk, kt, p, gates)


def _ml_out_kernel(hf_ref, hb_ref, o_ref, g_ref, out_ref):
    for h in range(ML_HEADS):
        ch = slice(h * ML_DV, (h + 1) * ML_DV)
        hh = hf_ref[0, :, ch] + hb_ref[0, :, ch]
        hh = hh * lax.rsqrt(jnp.mean(hh * hh, axis=-1, keepdims=True) + EPS)
        out_ref[:, ch] = (hh * g_ref[:, ch] * jax.nn.sigmoid(o_ref[:, ch])).astype(out_ref.dtype)


def ml_out(h2, p, gain, n_lat_blk):
    r = ROW_BLK
    return pl.pallas_call(
        _ml_out_kernel,
        grid=(n_lat_blk,),
        in_specs=[pl.BlockSpec((1, r, ML_V_W), lambda i: (0, i, 0)),
                  pl.BlockSpec((1, r, ML_V_W), lambda i: (1, i, 0)),
                  pl.BlockSpec((r, ML_V_W), lambda i: (i, COL_O // ML_V_W)),
                  pl.BlockSpec((1, ML_V_W), lambda i: (0, 0))],
        out_specs=pl.BlockSpec((r, ML_V_W), lambda i: (i, 0)),
        out_shape=jax.ShapeDtypeStruct((n_lat_blk * r, ML_V_W), jnp.bfloat16),
        compiler_params=_params("parallel"),
    )(h2, h2, p, gain[None])


def _merge_kernel(hm_ref, hh_ref, wm_ref, wh_ref, gm_ref, gh_ref, o_ref):
    ym = jnp.dot(hm_ref[...], wm_ref[...], preferred_element_type=jnp.float32)
    yh = jnp.dot(hh_ref[...], wh_ref[...], preferred_element_type=jnp.float32)
    o_ref[...] = (jax.nn.sigmoid(gm_ref[...]) * ym + jax.nn.sigmoid(gh_ref[...]) * yh).astype(o_ref.dtype)


def merge_branches(h_ml, h_hy, w_pm, w_ph, p, tm=512, tn=1024):
    L, K = h_ml.shape
    D = w_pm.shape[1]
    act = pl.BlockSpec((tm, K), lambda j, i: (i, 0))
    wgt = pl.BlockSpec((K, tn), lambda j, i: (0, j))
    return pl.pallas_call(
        _merge_kernel,
        grid=(D // tn, L // tm),
        in_specs=[act, act, wgt, wgt,
                  pl.BlockSpec((tm, tn), lambda j, i: (i, COL_BG // tn + j)),
                  pl.BlockSpec((tm, tn), lambda j, i: (i, (COL_BG + D) // tn + j))],
        out_specs=pl.BlockSpec((tm, tn), lambda j, i: (i, j)),
        out_shape=jax.ShapeDtypeStruct((L, D), jnp.bfloat16),
        compiler_params=_params("parallel", "parallel"),
    )(h_ml, h_hy, w_pm, w_ph, p, p)


def _out_proj_kernel(y_ref, w_ref, x_ref, g1_ref, ng_ref, sc_ref, sh_ref, h_ref, v_ref):
    h = x_ref[...] + g1_ref[...] * jnp.dot(y_ref[...], w_ref[...], preferred_element_type=jnp.float32)
    h_ref[...] = h
    n = h * lax.rsqrt(jnp.mean(h * h, axis=-1, keepdims=True) + EPS)
    v_ref[...] = (n * ng_ref[...] * (1.0 + sc_ref[...]) + sh_ref[...]).astype(v_ref.dtype)


def out_proj(y, w_o, x, g1, ng, sc, sh, tm=256):
    L, D = x.shape
    row = pl.BlockSpec((tm, D), lambda i: (i, 0))
    vec = pl.BlockSpec((1, D), lambda i: (0, 0))
    return pl.pallas_call(
        _out_proj_kernel,
        grid=(L // tm,),
        in_specs=[row, pl.BlockSpec((D, D), lambda i: (0, 0)), row, vec, vec, vec, vec],
        out_specs=[row, row],
        out_shape=[jax.ShapeDtypeStruct((L, D), jnp.float32), jax.ShapeDtypeStruct((L, D), jnp.bfloat16)],
        compiler_params=_params("parallel"),
    )(y, w_o, x, g1, ng, sc, sh)


def _final_kernel(h_ref, p_ref, g2_ref, fg_ref, o_ref):
    h = h_ref[...] + g2_ref[...] * p_ref[...]
    o_ref[...] = h * lax.rsqrt(jnp.mean(h * h, axis=-1, keepdims=True) + EPS) * fg_ref[...]


def final_norm(h, peer_out, g2, fg, tm=256):
    L, D = h.shape
    row = pl.BlockSpec((tm, D), lambda i: (i, 0))
    vec = pl.BlockSpec((1, D), lambda i: (0, 0))
    return pl.pallas_call(
        _final_kernel,
        grid=(L // tm,),
        in_specs=[row, row, vec, vec],
        out_specs=row,
        out_shape=jax.ShapeDtypeStruct((L, D), jnp.float32),
        compiler_params=_params("parallel"),
    )(h, peer_out, g2, fg)


def token_mixer_inputs(x, ctx, mods, norm_g, w_in):
    L = x.shape[0]
    assert ctx.shape[0] == ROW_BLK and L % ROW_BLK == 0
    n_lat_blk = L // ROW_BLK
    sh1, sc1 = mods[:, 0], mods[:, 1]
    u = norm_mod(x, ctx, norm_g, sc1[:, None, :], sh1[:, None, :])
    n_in = w_in.shape[1]
    w_r = jnp.concatenate([w_in[:, :COL_O + ML_V_W], w_in[:, COL_O + ML_V_W + 4 * ML_HEADS:],
                           w_in[:, COL_O + ML_V_W:COL_O + ML_V_W + 4 * ML_HEADS],
                           jnp.zeros((w_in.shape[0], IN_W_PAD - n_in), w_in.dtype)], axis=1).astype(jnp.bfloat16)
    return pmm(u, w_r, tm=768, tn=1536), n_lat_blk


def kernel(x, c, ctx, c_ctx, w_mod, b_mod, norm1_g, norm2_g, final_g, w_in, ml_conv_w, ml_gate_b, ml_norm_g, hy_conv_w, hy_w1, hy_b1, hy_w2, hy_b2, hy_w3, hy_freq, hy_bias, w_proj_ml, w_proj_hy, w_out, peer_wq, peer_keys, peer_u, peer_v):
    l = 0
    L, D = x.shape[1:]
    bf = jnp.bfloat16
    s_rows = jnp.zeros((8, D), jnp.float32).at[0].set(jax.nn.silu(c[0])).at[1].set(jax.nn.silu(c_ctx))
    mods = (pmm(s_rows, w_mod[l], tm=8, tn=1024)[:2] + b_mod[l]).reshape(2, 6, D)

    p, n_lat_blk = token_mixer_inputs(x[0], ctx[0], mods, norm1_g[l], w_in[l])
    T = p.shape[0]

    qk = qk_conv(p, ml_conv_w[l].reshape(9, 2 * ML_QK_W), n_lat_blk)
    kt = qk[:, ML_QK_W:].T
    gt = p[:, COL_GATE:COL_GATE + 4 * ML_HEADS].reshape(T, 4, ML_HEADS) + ml_gate_b[l]
    gates = jnp.stack([jnp.stack([gt[:, 0], jax.nn.log_sigmoid(gt[:, 1])]),
                       jnp.stack([gt[:, 2], jax.nn.log_sigmoid(gt[:, 3])])]).transpose(0, 3, 1, 2)
    h2 = mlstm_scan(qk, kt, p, gates, n_lat_blk)
    h_ml = ml_out(h2, p, ml_norm_g[l], n_lat_blk)

    xs = hy_conv(p, hy_conv_w[l], n_lat_blk)
    taps = hyena_filter_taps(L, hy_w1[l], hy_b1[l], hy_w2[l], hy_b2[l], hy_w3[l], hy_freq[l])
    h_hy = hyena(xs, taps, hy_bias[l])

    y = merge_branches(h_ml, h_hy.astype(bf), w_proj_ml[l].astype(bf), w_proj_hy[l].astype(bf), p)
    h_lat, v_lat = out_proj(y, w_out[l].astype(bf), x[0], mods[0:1, 2], norm2_g[l][None], mods[0:1, 4], mods[0:1, 3])
    peer_out = peer(v_lat, peer_wq[l], peer_keys[l], peer_u[l], peer_v[l])
    return final_norm(h_lat, peer_out, mods[0:1, 5], final_g[None])[None]
```

```python
import functools
import math
import numpy as np
import jax
import jax.numpy as jnp
from jax import lax
from jax.experimental import pallas as pl
from jax.experimental.pallas import tpu as pltpu

D_MODEL = 2048
GRID_W = 64
EPS = 1e-6

ML_HEADS = 8
ML_DQK = D_MODEL // 16
ML_DV = D_MODEL // 8
ML_QK_W = ML_HEADS * ML_DQK
ML_V_W = ML_HEADS * ML_DV

HY_W = D_MODEL
HY_ORDER = 2
HY_BANDS = 8
HY_SHIFT = 0.05
HY_MIN_DECAY = math.log(1e-2) / 1.5
HY_MAX_DECAY = math.log(1e-2) / 0.3

PEER_HEADS = 8
PEER_NKEYS = 128
PEER_DKEY = 256
PEER_TOPK = 16

LANES = 128
NEG_INF = float('-inf')
VMEM_LIMIT_BYTES = 56 * 1024 * 1024


def _mm_kernel(a_ref, b_ref, o_ref):
    o_ref[...] = jnp.dot(a_ref[...].astype(jnp.bfloat16), b_ref[...].astype(jnp.bfloat16),
                         preferred_element_type=jnp.float32).astype(o_ref.dtype)


def pmm(a, b, tm=512, tn=512, out_dtype=jnp.float32):
    M, K = a.shape
    _, N = b.shape
    tm = min(tm, M)
    tn = min(tn, N)
    assert M % tm == 0 and N % tn == 0, (M, N, tm, tn)
    return pl.pallas_call(
        _mm_kernel,
        grid=(N // tn, M // tm),
        in_specs=[pl.BlockSpec((tm, K), lambda j, i: (i, 0)),
                  pl.BlockSpec((K, tn), lambda j, i: (0, j))],
        out_specs=pl.BlockSpec((tm, tn), lambda j, i: (i, j)),
        out_shape=jax.ShapeDtypeStruct((M, N), out_dtype),
        compiler_params=pltpu.CompilerParams(
            dimension_semantics=("parallel", "parallel"),
            vmem_limit_bytes=VMEM_LIMIT_BYTES),
    )(a, b)


FFT_R = 128
FFT_N = FFT_R * FFT_R
HY_FEAT_PAD = 32
HIGHEST = lax.Precision.HIGHEST


def _dft_tables():
    r = np.arange(FFT_R)
    ang = 2.0 * np.pi * np.outer(r, r) / FFT_R
    c, s = np.cos(ang), np.sin(ang)
    first = np.concatenate([c, -s], axis=0)
    mid_fwd = np.block([[c, s], [-s, c]])
    mid_inv = np.block([[c, -s], [s, c]])
    last = np.concatenate([c, -s], axis=1) / FFT_N
    tw = 2.0 * np.pi * np.outer(r, r) / FFT_N
    f32 = lambda a: jnp.asarray(a, jnp.float32)
    return f32(first), f32(mid_fwd), f32(mid_inv), f32(last), f32(np.cos(tw)), f32(np.sin(tw))


def _fft_mid_kernel(a_ref, tc_ref, ts_ref, mf_ref, mi_ref, *rest, inverse):
    if inverse:
        g_ref, o_ref, b_ref = rest
    else:
        o_ref, b_ref = rest
    cb = a_ref.shape[-1]
    tc = tc_ref[0]
    ts = ts_ref[0]
    for lt in range(cb // LANES):
        ch = slice(lt * LANES, (lt + 1) * LANES)
        ar = a_ref[0, 0, :, ch]
        ai = a_ref[1, 0, :, ch]
        b_ref[0:FFT_R, ch] = (ar * tc + ai * ts).astype(b_ref.dtype)
        b_ref[FFT_R:, ch] = (ai * tc - ar * ts).astype(b_ref.dtype)
    x = jnp.dot(mf_ref[...], b_ref[...], preferred_element_type=jnp.float32)
    if not inverse:
        o_ref[0, 0] = x[:FFT_R].astype(o_ref.dtype)
        o_ref[1, 0] = x[FFT_R:].astype(o_ref.dtype)
        return
    xr, xi = x[:FFT_R], x[FFT_R:]
    gr = g_ref[0, 0].astype(jnp.float32)
    gi = g_ref[1, 0].astype(jnp.float32)
    b_ref[0:FFT_R, :] = (xr * gr - xi * gi).astype(b_ref.dtype)
    b_ref[FFT_R:, :] = (xr * gi + xi * gr).astype(b_ref.dtype)
    q = jnp.dot(mi_ref[...], b_ref[...], preferred_element_type=jnp.float32)
    for lt in range(cb // LANES):
        ch = slice(lt * LANES, (lt + 1) * LANES)
        qr = q[:FFT_R, ch]
        qi = q[FFT_R:, ch]
        o_ref[0, 0, :, ch] = (qr * tc - qi * ts).astype(o_ref.dtype)
        o_ref[1, 0, :, ch] = (qi * tc + qr * ts).astype(o_ref.dtype)


def fft_mid(a, tc, ts, mid_fwd, mid_inv, g=None, cb=2048, out_dtype=jnp.float32):
    C = a.shape[-1]
    cb = min(cb, C)
    inverse = g is not None
    blk = pl.BlockSpec((2, 1, FFT_R, cb), lambda k, j: (0, k, 0, j))
    tw = pl.BlockSpec((1, FFT_R, LANES), lambda k, j: (k, 0, 0))
    mat = pl.BlockSpec((2 * FFT_R, 2 * FFT_R), lambda k, j: (0, 0))
    return pl.pallas_call(
        functools.partial(_fft_mid_kernel, inverse=inverse),
        grid=(FFT_R, C // cb),
        in_specs=[blk, tw, tw, mat, mat] + ([blk] if inverse else []),
        out_specs=blk,
        out_shape=jax.ShapeDtypeStruct(a.shape, out_dtype),
        scratch_shapes=[pltpu.VMEM((2 * FFT_R, cb), jnp.bfloat16)],
        compiler_params=pltpu.CompilerParams(
            dimension_semantics=("parallel", "parallel"),
            vmem_limit_bytes=VMEM_LIMIT_BYTES),
    )(a, tc, ts, mid_fwd.astype(jnp.bfloat16), mid_inv.astype(jnp.bfloat16), *([g] if inverse else []))


FFT_GRP = 32


def _fft_first_kernel(x_ref, f_ref, o_ref):
    rows, cb = x_ref.shape
    n1 = rows // FFT_R
    kk = f_ref.shape[0]
    fb = jnp.broadcast_to(f_ref[...][None], (FFT_GRP, kk, n1))
    for g in range(FFT_R // FFT_GRP):
        n2 = slice(g * FFT_GRP, (g + 1) * FFT_GRP)
        xt = jnp.swapaxes(x_ref[...].reshape(n1, FFT_R, cb)[:, n2, :], 0, 1).astype(jnp.bfloat16)
        a = lax.dot_general(fb, xt, (((2,), (1,)), ((0,), (0,))), preferred_element_type=jnp.float32)
        o_ref[:, n2, :] = jnp.swapaxes(a, 0, 1).astype(o_ref.dtype)


def fft_first(x, rows, row_blk, col0, width, first, cb=128):
    n1 = rows // FFT_R
    return pl.pallas_call(
        _fft_first_kernel,
        grid=(width // cb,),
        in_specs=[pl.BlockSpec((rows, cb), lambda j: (row_blk, col0 // cb + j)),
                  pl.BlockSpec((2 * FFT_R, n1), lambda j: (0, 0))],
        out_specs=pl.BlockSpec((2 * FFT_R, FFT_R, cb), lambda j: (0, 0, j)),
        out_shape=jax.ShapeDtypeStruct((2 * FFT_R, FFT_R, width), jnp.bfloat16),
        compiler_params=pltpu.CompilerParams(
            dimension_semantics=("parallel",), vmem_limit_bytes=VMEM_LIMIT_BYTES),
    )(x, first[:, :n1].astype(jnp.bfloat16))


def _fft_last_kernel(q_ref, f_ref, z_ref, x_ref, b_ref, o_ref):
    kk, _, cb = q_ref.shape
    n1 = f_ref.shape[0]
    fb = jnp.broadcast_to(f_ref[...][None], (FFT_GRP, n1, kk))
    for g in range(FFT_R // FFT_GRP):
        qt = jnp.swapaxes(q_ref[:, g * FFT_GRP:(g + 1) * FFT_GRP, :].astype(jnp.float32), 0, 1)
        y = lax.dot_general(fb, qt.astype(jnp.bfloat16), (((2,), (1,)), ((0,), (0,))),
                            preferred_element_type=jnp.float32)
        yt = jnp.swapaxes(y, 0, 1)
        for i in range(n1):
            rows = slice(i * FFT_R + g * FFT_GRP, i * FFT_R + (g + 1) * FFT_GRP)
            o_ref[rows, :] = (x_ref[rows, :] * (yt[i] + z_ref[rows, :] * b_ref[...])).astype(o_ref.dtype)


def fft_last(q, last, z, z_col0, xg, xg_col0, bias, out_dtype, cb=128):
    C = q.shape[-1]
    L = z.shape[0]
    n1 = L // FFT_R
    return pl.pallas_call(
        _fft_last_kernel,
        grid=(C // cb,),
        in_specs=[pl.BlockSpec((2 * FFT_R, FFT_R, cb), lambda j: (0, 0, j)),
                  pl.BlockSpec((n1, 2 * FFT_R), lambda j: (0, 0)),
                  pl.BlockSpec((L, cb), lambda j: (0, z_col0 // cb + j)),
                  pl.BlockSpec((L, cb), lambda j: (0, xg_col0 // cb + j)),
                  pl.BlockSpec((1, cb), lambda j: (0, j))],
        out_specs=pl.BlockSpec((L, cb), lambda j: (0, j)),
        out_shape=jax.ShapeDtypeStruct((L, C), out_dtype),
        compiler_params=pltpu.CompilerParams(
            dimension_semantics=("parallel",), vmem_limit_bytes=VMEM_LIMIT_BYTES),
    )(q, last[:n1].astype(jnp.bfloat16), z, xg, bias)


def _hyena_filter_kernel(ft_ref, tn_ref, w1_ref, b1_ref, w2_ref, b2_ref, fr_ref, w3_ref, dl_ref, o_ref,
                         hdn_ref, win_ref):
    cols = o_ref.shape[-1]

    @pl.when(pl.program_id(1) == 0)
    def _():
        pre = jnp.dot(ft_ref[...], w1_ref[...], preferred_element_type=jnp.float32, precision=HIGHEST)
        hdn = jnp.sin(fr_ref[...] * (pre + b1_ref[...]))
        pre = jnp.dot(hdn, w2_ref[...], preferred_element_type=jnp.float32, precision=HIGHEST)
        hdn_ref[...] = jnp.sin(fr_ref[...] * (pre + b2_ref[...])).astype(hdn_ref.dtype)
        tn = tn_ref[...]
        for lt in range(cols // LANES):
            ch = slice(lt * LANES, (lt + 1) * LANES)
            win_ref[:, ch] = jnp.exp(-tn * dl_ref[:, ch]) + HY_SHIFT

    filt = jnp.dot(hdn_ref[...], w3_ref[0].astype(jnp.bfloat16), preferred_element_type=jnp.float32)
    o_ref[0] = filt * win_ref[...]


def hyena_filter_taps(L, w1, b1, w2, b2, w3, freq, tr=512):
    assert FFT_N == 2 * L
    n = np.arange(FFT_N)
    t = np.where(n < L, n, (FFT_N - n) % L).astype(np.float32)
    tnorm = t / np.float32(L)
    bands = np.linspace(1e-4, HY_BANDS - 1, HY_BANDS, dtype=np.float32)
    ang = (np.float32(2.0 * math.pi / L) * t[:, None] * bands[None, :]).astype(np.float64)
    feats = np.concatenate([tnorm[:, None], np.cos(ang), -np.sin(ang)], axis=-1)
    feats = np.pad(feats, ((0, 0), (0, HY_FEAT_PAD - feats.shape[1]))).astype(np.float32)
    tmark = np.broadcast_to(tnorm[:, None], (FFT_N, LANES))
    deltas = np.abs(np.linspace(HY_MIN_DECAY, HY_MAX_DECAY, HY_W, dtype=np.float32))[None, :]
    w1p = jnp.pad(w1, ((0, HY_FEAT_PAD - w1.shape[0]), (0, 0)))
    ffn = w2.shape[0]
    w3r = w3.reshape(ffn, 2 * HY_ORDER, HY_W).transpose(1, 0, 2)
    half = L // tr
    row = lambda r, o: (r, 0)
    fix = lambda r, o: (0, 0)
    return pl.pallas_call(
        _hyena_filter_kernel,
        grid=(FFT_N // tr, HY_ORDER),
        in_specs=[pl.BlockSpec((tr, HY_FEAT_PAD), row),
                  pl.BlockSpec((tr, LANES), row),
                  pl.BlockSpec((HY_FEAT_PAD, ffn), fix),
                  pl.BlockSpec((1, ffn), fix),
                  pl.BlockSpec((ffn, ffn), fix),
                  pl.BlockSpec((1, ffn), fix),
                  pl.BlockSpec((1, ffn), fix),
                  pl.BlockSpec((1, ffn, HY_W), lambda r, o: ((r // half) * HY_ORDER + o, 0, 0)),
                  pl.BlockSpec((1, HY_W), fix)],
        out_specs=pl.BlockSpec((1, tr, HY_W), lambda r, o: (o, r, 0)),
        out_shape=jax.ShapeDtypeStruct((HY_ORDER, FFT_N, HY_W), jnp.float32),
        scratch_shapes=[pltpu.VMEM((tr, ffn), jnp.bfloat16), pltpu.VMEM((tr, HY_W), jnp.float32)],
        compiler_params=pltpu.CompilerParams(
            dimension_semantics=("parallel", "arbitrary"), vmem_limit_bytes=VMEM_LIMIT_BYTES),
    )(jnp.asarray(feats), jnp.asarray(tmark), w1p, b1[None], w2, b2[None], freq[None], w3r, jnp.asarray(deltas))


def hyena(xs, taps, bias):
    L = xs.shape[0]
    C = xs.shape[1] // 3
    first, mid_fwd, mid_inv, last, tcos, tsin = _dft_tables()
    tc = jnp.broadcast_to(tcos[:, :, None], (FFT_R, FFT_R, LANES))
    ts = jnp.broadcast_to(tsin[:, :, None], (FFT_R, FFT_R, LANES))
    shape4 = (2, FFT_R, FFT_R, C)
    bf = jnp.bfloat16
    taps2 = taps.reshape(HY_ORDER * FFT_N, C)
    z, z_col0 = xs, 2 * C
    for o in range(HY_ORDER):
        ga = fft_first(taps2, FFT_N, o, 0, C, first)
        g = fft_mid(ga.reshape(shape4), tc, ts, mid_fwd, mid_inv, out_dtype=bf)
        a = fft_first(z, L, 0, z_col0, C, first)
        q = fft_mid(a.reshape(shape4), tc, ts, mid_fwd, mid_inv, g=g, out_dtype=bf)
        bias_eff = (bias[o] + taps[o, L])[None, :]
        z = fft_last(q.reshape(2 * FFT_R, FFT_R, C), last, z, z_col0, xs, o * C, bias_eff,
                     jnp.float32 if o + 1 < HY_ORDER else bf)
        z_col0 = 0
    return z


_CAND_ROWS = tuple(PEER_TOPK // (i + 1) for i in range(PEER_TOPK))


def _extract_top(s, rounds):
    rows = s.shape[0]
    iota = lax.broadcasted_iota(jnp.int32, s.shape, 0).astype(jnp.float32)
    tops = []
    for _ in range(rounds):
        m = jnp.max(s, axis=0, keepdims=True)
        first = jnp.min(jnp.where(s == m, iota, float(rows)), axis=0, keepdims=True)
        s = jnp.where(iota == first, NEG_INF, s)
        tops.append(m)
    return tops


def _peer_topk_kernel(q_ref, k_ref, s1_ref, s2_ref, e1_ref, e2_ref, thr_ref):
    tk = q_ref.shape[0]
    for u in range(tk // LANES):
        tok = slice(u * LANES, (u + 1) * LANES)
        s = []
        tops = []
        for p in range(2):
            qp = q_ref[tok, p * PEER_NKEYS:(p + 1) * PEER_NKEYS]
            sp = lax.dot_general(k_ref[0, p], qp, (((1,), (1,)), ((), ())),
                                 preferred_element_type=jnp.float32)
            s.append(sp)
            tops.append(_extract_top(sp, PEER_TOPK))
        top2a = jnp.concatenate(tops[1][:8], axis=0)
        top2b = jnp.concatenate(tops[1][8:], axis=0)
        row = lax.broadcasted_iota(jnp.int32, (8, LANES), 0)
        cands = [tops[0][0] + top2a, tops[0][0] + top2b]
        for i in range(1, PEER_TOPK):
            cands.append(jnp.where(row < _CAND_ROWS[i], tops[0][i] + top2a, NEG_INF))
        best = _extract_top(jnp.concatenate(cands, axis=0), PEER_TOPK)
        z = jnp.ones_like(best[0])
        for b in best[1:]:
            z = z + jnp.exp(b - best[0])
        s1_ref[0, :, tok] = s[0]
        s2_ref[0, :, tok] = s[1]
        e1_ref[0, :, tok] = jnp.exp(s[0] - tops[0][0]) / z
        e2_ref[0, :, tok] = jnp.exp(s[1] - tops[1][0])
        thr_ref[0, :, tok] = best[-1]


def peer_topk(q, sub_keys, tk=256):
    T = q.shape[0]
    H = PEER_HEADS
    big = jax.ShapeDtypeStruct((H, PEER_NKEYS, T), jnp.float32)
    big_spec = pl.BlockSpec((1, PEER_NKEYS, tk), lambda i, h: (h, 0, i))
    s1, s2, e1, e2, thr = pl.pallas_call(
        _peer_topk_kernel,
        grid=(T // tk, H),
        in_specs=[pl.BlockSpec((tk, PEER_DKEY), lambda i, h: (i, h)),
                  pl.BlockSpec((1, 2, PEER_NKEYS, PEER_DKEY // 2), lambda i, h: (h, 0, 0, 0))],
        out_specs=[big_spec, big_spec, big_spec, big_spec,
                   pl.BlockSpec((1, 1, tk), lambda i, h: (h, 0, i))],
        out_shape=[big, big, big, big, jax.ShapeDtypeStruct((H, 1, T), jnp.float32)],
        compiler_params=pltpu.CompilerParams(
            dimension_semantics=("parallel", "parallel"),
            vmem_limit_bytes=VMEM_LIMIT_BYTES),
    )(q, sub_keys.astype(jnp.bfloat16))
    return s1, s2, e1, e2, thr.reshape(H, T)


def _peer_expert_kernel(v_ref, u_ref, vt_ref, s1a_ref, e1a_ref, s2_ref, e2_ref, thr_ref, o_ref, sc_ref, w_ref):
    j = pl.program_id(1)
    tt = v_ref.shape[0]
    na = u_ref.shape[0] // PEER_NKEYS

    def scores():
        sc_ref[...] = lax.dot_general(u_ref[...], v_ref[...], (((1,), (1,)), ((), ())),
                                      preferred_element_type=jnp.float32)

    def weights(al, w_ref):
        rows = pl.ds(pl.multiple_of(al * PEER_NKEYS, PEER_NKEYS), PEER_NKEYS)
        for u in range(tt // LANES):
            tok = slice(u * LANES, (u + 1) * LANES)
            g = jnp.zeros((PEER_NKEYS, LANES), jnp.float32)
            for h in range(PEER_HEADS):
                s1row = s1a_ref[al, h:h + 1, tok]
                e1row = e1a_ref[al, h:h + 1, tok]
                keep = (s1row + s2_ref[h, :, tok]) >= thr_ref[h:h + 1, tok]
                g = g + jnp.where(keep, e1row * e2_ref[h, :, tok], 0.0)
            sc = sc_ref[rows, tok]
            act = 0.5 * sc * (1.0 + lax.erf(sc * math.sqrt(0.5)))
            w_ref[rows, tok] = (act * g).astype(w_ref.dtype)

    @pl.when(j == 0)
    def _():
        o_ref[...] = jnp.zeros_like(o_ref)

    scores()

    def body(al, carry):
        weights(al, w_ref)
        return carry
    lax.fori_loop(0, na, body, 0)
    o_ref[...] += jnp.dot(vt_ref[...], w_ref[...], preferred_element_type=jnp.float32)


def peer_experts(v, expert_u, expert_vt, s1a, e1a, s2, e2, thr, tt=512, nb=1024):
    T, D = v.shape
    N = expert_u.shape[0]
    na = nb // PEER_NKEYS
    H = PEER_HEADS
    return pl.pallas_call(
        _peer_expert_kernel,
        grid=(T // tt, N // nb),
        in_specs=[pl.BlockSpec((tt, D), lambda i, j: (i, 0)),
                  pl.BlockSpec((nb, D), lambda i, j: (j, 0)),
                  pl.BlockSpec((D, nb), lambda i, j: (0, j)),
                  pl.BlockSpec((na, H, tt), lambda i, j: (j, 0, i)),
                  pl.BlockSpec((na, H, tt), lambda i, j: (j, 0, i)),
                  pl.BlockSpec((H, PEER_NKEYS, tt), lambda i, j: (0, 0, i)),
                  pl.BlockSpec((H, PEER_NKEYS, tt), lambda i, j: (0, 0, i)),
                  pl.BlockSpec((H, tt), lambda i, j: (0, i))],
        out_specs=pl.BlockSpec((D, tt), lambda i, j: (0, i)),
        out_shape=jax.ShapeDtypeStruct((D, T), jnp.float32),
        scratch_shapes=[pltpu.VMEM((nb, tt), jnp.float32), pltpu.VMEM((nb, tt), jnp.bfloat16)],
        compiler_params=pltpu.CompilerParams(
            dimension_semantics=("parallel", "arbitrary"),
            vmem_limit_bytes=VMEM_LIMIT_BYTES),
    )(v, expert_u, expert_vt, s1a, e1a, s2, e2, thr)


def peer(u, w_q, sub_keys, expert_u, expert_v, tk=256, tt=512, nb=1024):
    ub = u.astype(jnp.bfloat16)
    q = pmm(ub, w_q.astype(jnp.bfloat16), out_dtype=jnp.bfloat16)
    s1, s2, e1, e2, thr = peer_topk(q, sub_keys, tk=tk)
    out_t = peer_experts(ub, expert_u.astype(jnp.bfloat16), expert_v.astype(jnp.bfloat16).T,
                         s1.transpose(1, 0, 2), e1.transpose(1, 0, 2), s2, e2, thr, tt=tt, nb=nb)
    return out_t.T


ROW_BLK = 256
STATE_MIN = -1e30

COL_QK, COL_V, COL_O = 0, 2 * ML_QK_W, 2 * ML_QK_W + ML_V_W
COL_HY = COL_O + ML_V_W
COL_BG = COL_HY + 3 * HY_W
COL_GATE = COL_BG + 2 * D_MODEL
IN_W_PAD = 16896


def _params(*sem):
    return pltpu.CompilerParams(dimension_semantics=sem, vmem_limit_bytes=VMEM_LIMIT_BYTES)


def _norm_mod_kernel(x_ref, c_ref, g_ref, sc_ref, sh_ref, o_ref, *, n_lat_blk):
    def emit(x):
        y = x * lax.rsqrt(jnp.mean(x * x, axis=-1, keepdims=True) + EPS)
        o_ref[...] = (y * g_ref[...] * (1.0 + sc_ref[0]) + sh_ref[0]).astype(o_ref.dtype)

    @pl.when(pl.program_id(0) < n_lat_blk)
    def _():
        emit(x_ref[...])

    @pl.when(pl.program_id(0) >= n_lat_blk)
    def _():
        emit(c_ref[...])


def norm_mod(x, ctx, g, sc, sh):
    L, D = x.shape
    n_lat_blk = L // ROW_BLK
    mod = pl.BlockSpec((1, 1, D), lambda i: (i // n_lat_blk, 0, 0))
    return pl.pallas_call(
        functools.partial(_norm_mod_kernel, n_lat_blk=n_lat_blk),
        grid=(n_lat_blk + 1,),
        in_specs=[pl.BlockSpec((ROW_BLK, D), lambda i: (jnp.minimum(i, n_lat_blk - 1), 0)),
                  pl.BlockSpec((ROW_BLK, D), lambda i: (0, 0)),
                  pl.BlockSpec((1, D), lambda i: (0, 0)), mod, mod],
        out_specs=pl.BlockSpec((ROW_BLK, D), lambda i: (i, 0)),
        out_shape=jax.ShapeDtypeStruct((L + ROW_BLK, D), jnp.bfloat16),
        compiler_params=_params("parallel"),
    )(x, ctx, g[None], sc, sh)


def _fill_halo(buf_ref, prev_ref, cur_ref, next_ref):
    r = cur_ref.shape[0]
    h = prev_ref.shape[0]
    buf_ref[0:h, :] = prev_ref[...]
    buf_ref[h:h + r, :] = cur_ref[...]
    buf_ref[h + r:2 * h + r, :] = next_ref[...]
    return h


def _qk_conv_kernel(prev_ref, cur_ref, next_ref, w_ref, o_ref, buf_ref, *, n_lat_blk, q_slabs):
    j = pl.program_id(0)
    i = pl.program_id(1)
    r, cw = cur_ref.shape
    halo = _fill_halo(buf_ref, prev_ref, cur_ref, next_ref)
    local = lax.broadcasted_iota(jnp.int32, (r, cw), 0)
    n_lat = n_lat_blk * r

    def tap(off):
        return buf_ref[pl.ds(halo + off, r), :]

    def finish(acc):
        y = acc * jax.nn.sigmoid(acc)
        scale = jnp.where(j < q_slabs, ML_DQK ** -0.5, 1.0)
        o_ref[...] = (y * scale).astype(o_ref.dtype)

    @pl.when(i < n_lat_blk)
    def _():
        g = i * r + local
        col = jnp.bitwise_and(g, GRID_W - 1)
        acc = jnp.zeros((r, cw), jnp.float32)
        for dr in (-1, 0, 1):
            for dc in (-1, 0, 1):
                t = tap(GRID_W * dr + dc)
                if dr == -1:
                    t = jnp.where(g >= GRID_W, t, 0.0)
                if dr == 1:
                    t = jnp.where(g < n_lat - GRID_W, t, 0.0)
                if dc == -1:
                    t = jnp.where(col >= 1, t, 0.0)
                if dc == 1:
                    t = jnp.where(col < GRID_W - 1, t, 0.0)
                acc = acc + t * w_ref[3 * (dr + 1) + dc + 1:3 * (dr + 1) + dc + 2, :]
        finish(acc)

    @pl.when(i >= n_lat_blk)
    def _():
        acc = tap(0) * w_ref[4:5, :]
        acc = acc + jnp.where(local >= 1, tap(-1), 0.0) * w_ref[3:4, :]
        acc = acc + jnp.where(local < r - 1, tap(1), 0.0) * w_ref[5:6, :]
        finish(acc)


def _halo_specs(cw, col0_blk, nblk, halo):
    per = ROW_BLK // halo
    cur = lambda j, i: (i, col0_blk + j)
    prev = lambda j, i: (jnp.maximum(i * per - 1, 0), col0_blk + j)
    nxt = lambda j, i: (jnp.minimum((i + 1) * per, nblk * per - 1), col0_blk + j)
    return [pl.BlockSpec((halo, cw), prev), pl.BlockSpec((ROW_BLK, cw), cur), pl.BlockSpec((halo, cw), nxt)]


QK_HALO = 128
SEQ_HALO = 8


def qk_conv(p, w9, n_lat_blk, cw=256):
    T = p.shape[0]
    nblk = T // ROW_BLK
    width = 2 * ML_QK_W
    return pl.pallas_call(
        functools.partial(_qk_conv_kernel, n_lat_blk=n_lat_blk, q_slabs=ML_QK_W // cw),
        grid=(width // cw, nblk),
        in_specs=_halo_specs(cw, COL_QK // cw, nblk, QK_HALO) + [pl.BlockSpec((9, cw), lambda j, i: (0, j))],
        out_specs=pl.BlockSpec((ROW_BLK, cw), lambda j, i: (i, j)),
        out_shape=jax.ShapeDtypeStruct((T, width), jnp.bfloat16),
        scratch_shapes=[pltpu.VMEM((ROW_BLK + 2 * QK_HALO, cw), jnp.float32)],
        compiler_params=_params("parallel", "parallel"),
    )(p, p, p, w9)


def _seq_conv_kernel(prev_ref, cur_ref, next_ref, w_ref, o_ref, buf_ref, *, n_rows):
    i = pl.program_id(1)
    r, cw = cur_ref.shape
    halo = _fill_halo(buf_ref, prev_ref, cur_ref, next_ref)
    g = i * r + lax.broadcasted_iota(jnp.int32, (r, cw), 0)
    acc = buf_ref[pl.ds(halo, r), :] * w_ref[1:2, :]
    acc = acc + jnp.where(g >= 1, buf_ref[pl.ds(halo - 1, r), :], 0.0) * w_ref[0:1, :]
    acc = acc + jnp.where(g < n_rows - 1, buf_ref[pl.ds(halo + 1, r), :], 0.0) * w_ref[2:3, :]
    o_ref[...] = acc.astype(o_ref.dtype)


def hy_conv(p, w3, n_lat_blk, cw=512):
    nblk = p.shape[0] // ROW_BLK
    width = 3 * HY_W
    n_rows = n_lat_blk * ROW_BLK
    return pl.pallas_call(
        functools.partial(_seq_conv_kernel, n_rows=n_rows),
        grid=(width // cw, n_lat_blk),
        in_specs=_halo_specs(cw, COL_HY // cw, nblk, SEQ_HALO) + [pl.BlockSpec((3, cw), lambda j, i: (0, j))],
        out_specs=pl.BlockSpec((ROW_BLK, cw), lambda j, i: (i, j)),
        out_shape=jax.ShapeDtypeStruct((n_rows, width), jnp.float32),
        scratch_shapes=[pltpu.VMEM((ROW_BLK + 2 * SEQ_HALO, cw), jnp.float32)],
        compiler_params=_params("parallel", "parallel"),
    )(p, p, p, w3)


def _mlstm_kernel(q_ref, k_ref, kt_ref, v_ref, g_ref, o_ref, c_ref, n_ref, m_ref):
    d = pl.program_id(0)
    step = pl.program_id(2)
    r = q_ref.shape[0]

    @pl.when(step == 0)
    def _():
        c_ref[...] = jnp.zeros_like(c_ref)
        n_ref[...] = jnp.zeros_like(n_ref)
        m_ref[...] = jnp.full_like(m_ref, STATE_MIN)

    ig = g_ref[0, 0, 0:1, :]
    fg = g_ref[0, 0, 1:2, :]
    t_i = lax.broadcasted_iota(jnp.int32, (r, r), 0)
    s_i = lax.broadcasted_iota(jnp.int32, (r, r), 1)
    sgn = jnp.where(d == 0, 1, -1)
    reads = (s_i - t_i) * sgn <= 0
    read_by = (t_i - s_i) * sgn <= 0
    eye = s_i == t_i
    f_col = jnp.sum(jnp.where(eye, fg, 0.0), axis=1, keepdims=True)
    cum_col = jnp.sum(jnp.where(reads, fg, 0.0), axis=1, keepdims=True)
    cum_row = jnp.sum(jnp.where(read_by, f_col, 0.0), axis=0, keepdims=True)
    total = jnp.sum(fg, axis=1, keepdims=True)
    m_old = m_ref[...]

    dmat = jnp.where(reads, (cum_col - cum_row) + ig, NEG_INF)
    inter = cum_col + m_old
    m_t = jnp.maximum(inter, jnp.max(dmat, axis=1, keepdims=True))
    w = jnp.exp(dmat - m_t)
    s_inter = jnp.exp(inter - m_t)
    q = q_ref[...]
    v = v_ref[...]
    qkw = jnp.dot(q, kt_ref[...], preferred_element_type=jnp.float32) * w
    qn = jnp.sum(q.astype(jnp.float32) * n_ref[...], axis=1, keepdims=True)
    den = jnp.sum(qkw, axis=1, keepdims=True) + s_inter * qn
    num = jnp.dot(qkw.astype(jnp.bfloat16), v.astype(jnp.bfloat16), preferred_element_type=jnp.float32) \
        + s_inter * jnp.dot(q, c_ref[...].astype(jnp.bfloat16), preferred_element_type=jnp.float32)
    o_ref[0] = num / jnp.maximum(jnp.abs(den), jnp.exp(-m_t))

    dec = (total - cum_row) + ig
    m_new = jnp.maximum(total + m_old, jnp.max(dec, axis=1, keepdims=True))
    wk_row = jnp.exp(dec - m_new)
    s_old = jnp.exp(total + m_old - m_new)
    wk_col = jnp.sum(jnp.where(eye, wk_row, 0.0), axis=1, keepdims=True)
    c_ref[...] = s_old * c_ref[...] + jnp.dot(kt_ref[...], (wk_col * v).astype(jnp.bfloat16),
                                              preferred_element_type=jnp.float32)
    n_ref[...] = s_old * n_ref[...] + jnp.sum(wk_col * k_ref[...].astype(jnp.float32), axis=0, keepdims=True)
    m_ref[...] = m_new


def mlstm_scan(qk, kt, p, gates, n_lat_blk):
    T = qk.shape[0]
    r = ROW_BLK

    def blk(d, s):
        lat = jnp.where(d == 0, s - 1, n_lat_blk - s)
        return jnp.where(s == 0, n_lat_blk, lat)

    return pl.pallas_call(
        _mlstm_kernel,
        grid=(2, ML_HEADS, n_lat_blk + 1),
        in_specs=[pl.BlockSpec((r, ML_DQK), lambda d, h, s: (blk(d, s), h)),
                  pl.BlockSpec((r, ML_DQK), lambda d, h, s: (blk(d, s), ML_HEADS + h)),
                  pl.BlockSpec((ML_DQK, r), lambda d, h, s: (h, blk(d, s))),
                  pl.BlockSpec((r, ML_DV), lambda d, h, s: (blk(d, s), COL_V // ML_DV + h)),
                  pl.BlockSpec((1, 1, 2, r), lambda d, h, s: (d, h, 0, blk(d, s)))],
        out_specs=pl.BlockSpec((1, r, ML_DV), lambda d, h, s: (d, blk(d, s), h)),
        out_shape=jax.ShapeDtypeStruct((2, T, ML_V_W), jnp.float32),
        scratch_shapes=[pltpu.VMEM((ML_DQK, ML_DV), jnp.float32), pltpu.VMEM((1, ML_DQK), jnp.float32),
                        pltpu.VMEM((1, 1), jnp.float32)],
        compiler_params=_params("parallel", "parallel", "arbitrary"),
    )(qk, qk, kt, p, gates)


def _ml_out_kernel(hf_ref, hb_ref, o_ref, g_ref, out_ref):
    for h in range(ML_HEADS):
        ch = slice(h * ML_DV, (h + 1) * ML_DV)
        hh = hf_ref[0, :, ch] + hb_ref[0, :, ch]
        hh = hh * lax.rsqrt(jnp.mean(hh * hh, axis=-1, keepdims=True) + EPS)
        out_ref[:, ch] = (hh * g_ref[:, ch] * jax.nn.sigmoid(o_ref[:, ch])).astype(out_ref.dtype)


def ml_out(h2, p, gain, n_lat_blk):
    r = ROW_BLK
    return pl.pallas_call(
        _ml_out_kernel,
        grid=(n_lat_blk,),
        in_specs=[pl.BlockSpec((1, r, ML_V_W), lambda i: (0, i, 0)),
                  pl.BlockSpec((1, r, ML_V_W), lambda i: (1, i, 0)),
                  pl.BlockSpec((r, ML_V_W), lambda i: (i, COL_O // ML_V_W)),
                  pl.BlockSpec((1, ML_V_W), lambda i: (0, 0))],
        out_specs=pl.BlockSpec((r, ML_V_W), lambda i: (i, 0)),
        out_shape=jax.ShapeDtypeStruct((n_lat_blk * r, ML_V_W), jnp.bfloat16),
        compiler_params=_params("parallel"),
    )(h2, h2, p, gain[None])


def _merge_kernel(hm_ref, hh_ref, wm_ref, wh_ref, gm_ref, gh_ref, o_ref):
    ym = jnp.dot(hm_ref[...], wm_ref[...], preferred_element_type=jnp.float32)
    yh = jnp.dot(hh_ref[...], wh_ref[...], preferred_element_type=jnp.float32)
    o_ref[...] = (jax.nn.sigmoid(gm_ref[...]) * ym + jax.nn.sigmoid(gh_ref[...]) * yh).astype(o_ref.dtype)


def merge_branches(h_ml, h_hy, w_pm, w_ph, p, tm=512, tn=1024):
    L, K = h_ml.shape
    D = w_pm.shape[1]
    act = pl.BlockSpec((tm, K), lambda j, i: (i, 0))
    wgt = pl.BlockSpec((K, tn), lambda j, i: (0, j))
    return pl.pallas_call(
        _merge_kernel,
        grid=(D // tn, L // tm),
        in_specs=[act, act, wgt, wgt,
                  pl.BlockSpec((tm, tn), lambda j, i: (i, COL_BG // tn + j)),
                  pl.BlockSpec((tm, tn), lambda j, i: (i, (COL_BG + D) // tn + j))],
        out_specs=pl.BlockSpec((tm, tn), lambda j, i: (i, j)),
        out_shape=jax.ShapeDtypeStruct((L, D), jnp.bfloat16),
        compiler_params=_params("parallel", "parallel"),
    )(h_ml, h_hy, w_pm, w_ph, p, p)


def _out_proj_kernel(y_ref, w_ref, x_ref, g1_ref, ng_ref, sc_ref, sh_ref, h_ref, v_ref):
    h = x_ref[...] + g1_ref[...] * jnp.dot(y_ref[...], w_ref[...], preferred_element_type=jnp.float32)
    h_ref[...] = h
    n = h * lax.rsqrt(jnp.mean(h * h, axis=-1, keepdims=True) + EPS)
    v_ref[...] = (n * ng_ref[...] * (1.0 + sc_ref[...]) + sh_ref[...]).astype(v_ref.dtype)


def out_proj(y, w_o, x, g1, ng, sc, sh, tm=256):
    L, D = x.shape
    row = pl.BlockSpec((tm, D), lambda i: (i, 0))
    vec = pl.BlockSpec((1, D), lambda i: (0, 0))
    return pl.pallas_call(
        _out_proj_kernel,
        grid=(L // tm,),
        in_specs=[row, pl.BlockSpec((D, D), lambda i: (0, 0)), row, vec, vec, vec, vec],
        out_specs=[row, row],
        out_shape=[jax.ShapeDtypeStruct((L, D), jnp.float32), jax.ShapeDtypeStruct((L, D), jnp.bfloat16)],
        compiler_params=_params("parallel"),
    )(y, w_o, x, g1, ng, sc, sh)


def _final_kernel(h_ref, p_ref, g2_ref, fg_ref, o_ref):
    h = h_ref[...] + g2_ref[...] * p_ref[...]
    o_ref[...] = h * lax.rsqrt(jnp.mean(h * h, axis=-1, keepdims=True) + EPS) * fg_ref[...]


def final_norm(h, peer_out, g2, fg, tm=256):
    L, D = h.shape
    row = pl.BlockSpec((tm, D), lambda i: (i, 0))
    vec = pl.BlockSpec((1, D), lambda i: (0, 0))
    return pl.pallas_call(
        _final_kernel,
        grid=(L // tm,),
        in_specs=[row, row, vec, vec],
        out_specs=row,
        out_shape=jax.ShapeDtypeStruct((L, D), jnp.float32),
        compiler_params=_params("parallel"),
    )(h, peer_out, g2, fg)


def token_mixer_inputs(x, ctx, mods, norm_g, w_in):
    L = x.shape[0]
    assert ctx.shape[0] == ROW_BLK and L % ROW_BLK == 0
    n_lat_blk = L // ROW_BLK
    sh1, sc1 = mods[:, 0], mods[:, 1]
    u = norm_mod(x, ctx, norm_g, sc1[:, None, :], sh1[:, None, :])
    n_in = w_in.shape[1]
    w_r = jnp.concatenate([w_in[:, :COL_O + ML_V_W], w_in[:, COL_O + ML_V_W + 4 * ML_HEADS:],
                           w_in[:, COL_O + ML_V_W:COL_O + ML_V_W + 4 * ML_HEADS],
                           jnp.zeros((w_in.shape[0], IN_W_PAD - n_in), w_in.dtype)], axis=1).astype(jnp.bfloat16)
    return pmm(u, w_r, tm=768, tn=1536), n_lat_blk


def kernel(x, c, ctx, c_ctx, w_mod, b_mod, norm1_g, norm2_g, final_g, w_in, ml_conv_w, ml_gate_b, ml_norm_g, hy_conv_w, hy_w1, hy_b1, hy_w2, hy_b2, hy_w3, hy_freq, hy_bias, w_proj_ml, w_proj_hy, w_out, peer_wq, peer_keys, peer_u, peer_v):
    l = 0
    L, D = x.shape[1:]
    bf = jnp.bfloat16
    s_rows = jnp.zeros((8, D), jnp.float32).at[0].set(jax.nn.silu(c[0])).at[1].set(jax.nn.silu(c_ctx))
    mods = (pmm(s_rows, w_mod[l], tm=8, tn=1024)[:2] + b_mod[l]).reshape(2, 6, D)

    p, n_lat_blk = token_mixer_inputs(x[0], ctx[0], mods, norm1_g[l], w_in[l])
    T = p.shape[0]

    qk = qk_conv(p, ml_conv_w[l].reshape(9, 2 * ML_QK_W), n_lat_blk)
    kt = qk[:, ML_QK_W:].T
    gt = p[:, COL_GATE:COL_GATE + 4 * ML_HEADS].reshape(T, 4, ML_HEADS) + ml_gate_b[l]
    gates = jnp.stack([jnp.stack([gt[:, 0], jax.nn.log_sigmoid(gt[:, 1])]),
                       jnp.stack([gt[:, 2], jax.nn.log_sigmoid(gt[:, 3])])]).transpose(0, 3, 1, 2)
    h2 = mlstm_scan(qk, kt, p, gates, n_lat_blk)
    h_ml = ml_out(h2, p, ml_norm_g[l], n_lat_blk)

    xs = hy_conv(p, hy_conv_w[l], n_lat_blk)
    taps = hyena_filter_taps(L, hy_w1[l], hy_b1[l], hy_w2[l], hy_b2[l], hy_w3[l], hy_freq[l])
    h_hy = hyena(xs, taps, hy_bias[l])

    y = merge_branches(h_ml, h_hy.astype(bf), w_proj_ml[l].astype(bf), w_proj_hy[l].astype(bf), p)
    h_lat, v_lat = out_proj(y, w_out[l].astype(bf), x[0], mods[0:1, 2], norm2_g[l][None], mods[0:1, 4], mods[0:1, 3])
    peer_out = peer(v_lat, peer_wq[l], peer_keys[l], peer_u[l], peer_v[l])
    return final_norm(h_lat, peer_out, mods[0:1, 5], final_g[None])[None]
```

```python
import functools
import math
import numpy as np
import jax
import jax.numpy as jnp
from jax import lax
from jax.experimental import pallas as pl
from jax.experimental.pallas import tpu as pltpu

D_MODEL = 2048
GRID_W = 64
EPS = 1e-6

ML_HEADS = 8
ML_DQK = D_MODEL // 16
ML_DV = D_MODEL // 8
ML_QK_W = ML_HEADS * ML_DQK
ML_V_W = ML_HEADS * ML_DV

HY_W = D_MODEL
HY_ORDER = 2
HY_BANDS = 8
HY_SHIFT = 0.05
HY_MIN_DECAY = math.log(1e-2) / 1.5
HY_MAX_DECAY = math.log(1e-2) / 0.3

PEER_HEADS = 8
PEER_NKEYS = 128
PEER_DKEY = 256
PEER_TOPK = 16

LANES = 128
NEG_INF = float('-inf')
VMEM_LIMIT_BYTES = 56 * 1024 * 1024


def _mm_kernel(a_ref, b_ref, o_ref):
    o_ref[...] = jnp.dot(a_ref[...].astype(jnp.bfloat16), b_ref[...].astype(jnp.bfloat16),
                         preferred_element_type=jnp.float32).astype(o_ref.dtype)


def pmm(a, b, tm=512, tn=512, out_dtype=jnp.float32):
    M, K = a.shape
    _, N = b.shape
    tm = min(tm, M)
    tn = min(tn, N)
    assert M % tm == 0 and N % tn == 0, (M, N, tm, tn)
    return pl.pallas_call(
        _mm_kernel,
        grid=(N // tn, M // tm),
        in_specs=[pl.BlockSpec((tm, K), lambda j, i: (i, 0)),
                  pl.BlockSpec((K, tn), lambda j, i: (0, j))],
        out_specs=pl.BlockSpec((tm, tn), lambda j, i: (i, j)),
        out_shape=jax.ShapeDtypeStruct((M, N), out_dtype),
        compiler_params=pltpu.CompilerParams(
            dimension_semantics=("parallel", "parallel"),
            vmem_limit_bytes=VMEM_LIMIT_BYTES),
    )(a, b)


FFT_R = 128
FFT_N = FFT_R * FFT_R
HY_FEAT_PAD = 32
HIGHEST = lax.Precision.HIGHEST


def _dft_tables():
    r = np.arange(FFT_R)
    ang = 2.0 * np.pi * np.outer(r, r) / FFT_R
    c, s = np.cos(ang), np.sin(ang)
    first = np.concatenate([c, -s], axis=0)
    mid_fwd = np.block([[c, s], [-s, c]])
    mid_inv = np.block([[c, -s], [s, c]])
    last = np.concatenate([c, -s], axis=1) / FFT_N
    tw = 2.0 * np.pi * np.outer(r, r) / FFT_N
    f32 = lambda a: jnp.asarray(a, jnp.float32)
    return f32(first), f32(mid_fwd), f32(mid_inv), f32(last), f32(np.cos(tw)), f32(np.sin(tw))


def _fft_mid_kernel(a_ref, tc_ref, ts_ref, mf_ref, mi_ref, *rest, inverse):
    if inverse:
        g_ref, o_ref, b_ref = rest
    else:
        o_ref, b_ref = rest
    cb = a_ref.shape[-1]
    tc = tc_ref[0]
    ts = ts_ref[0]
    for lt in range(cb // LANES):
        ch = slice(lt * LANES, (lt + 1) * LANES)
        ar = a_ref[0, 0, :, ch]
        ai = a_ref[1, 0, :, ch]
        b_ref[0:FFT_R, ch] = (ar * tc + ai * ts).astype(b_ref.dtype)
        b_ref[FFT_R:, ch] = (ai * tc - ar * ts).astype(b_ref.dtype)
    x = jnp.dot(mf_ref[...], b_ref[...], preferred_element_type=jnp.float32)
    if not inverse:
        o_ref[0, 0] = x[:FFT_R].astype(o_ref.dtype)
        o_ref[1, 0] = x[FFT_R:].astype(o_ref.dtype)
        return
    xr, xi = x[:FFT_R], x[FFT_R:]
    gr = g_ref[0, 0].astype(jnp.float32)
    gi = g_ref[1, 0].astype(jnp.float32)
    b_ref[0:FFT_R, :] = (xr * gr - xi * gi).astype(b_ref.dtype)
    b_ref[FFT_R:, :] = (xr * gi + xi * gr).astype(b_ref.dtype)
    q = jnp.dot(mi_ref[...], b_ref[...], preferred_element_type=jnp.float32)
    for lt in range(cb // LANES):
        ch = slice(lt * LANES, (lt + 1) * LANES)
        qr = q[:FFT_R, ch]
        qi = q[FFT_R:, ch]
        o_ref[0, 0, :, ch] = (qr * tc - qi * ts).astype(o_ref.dtype)
        o_ref[1, 0, :, ch] = (qi * tc + qr * ts).astype(o_ref.dtype)


def fft_mid(a, tc, ts, mid_fwd, mid_inv, g=None, cb=2048, out_dtype=jnp.float32):
    C = a.shape[-1]
    cb = min(cb, C)
    inverse = g is not None
    blk = pl.BlockSpec((2, 1, FFT_R, cb), lambda k, j: (0, k, 0, j))
    tw = pl.BlockSpec((1, FFT_R, LANES), lambda k, j: (k, 0, 0))
    mat = pl.BlockSpec((2 * FFT_R, 2 * FFT_R), lambda k, j: (0, 0))
    return pl.pallas_call(
        functools.partial(_fft_mid_kernel, inverse=inverse),
        grid=(FFT_R, C // cb),
        in_specs=[blk, tw, tw, mat, mat] + ([blk] if inverse else []),
        out_specs=blk,
        out_shape=jax.ShapeDtypeStruct(a.shape, out_dtype),
        scratch_shapes=[pltpu.VMEM((2 * FFT_R, cb), jnp.bfloat16)],
        compiler_params=pltpu.CompilerParams(
            dimension_semantics=("parallel", "parallel"),
            vmem_limit_bytes=VMEM_LIMIT_BYTES),
    )(a, tc, ts, mid_fwd.astype(jnp.bfloat16), mid_inv.astype(jnp.bfloat16), *([g] if inverse else []))


FFT_GRP = 32


def _fft_first_kernel(x_ref, f_ref, o_ref):
    rows, cb = x_ref.shape
    n1 = rows // FFT_R
    kk = f_ref.shape[0]
    fb = jnp.broadcast_to(f_ref[...][None], (FFT_GRP, kk, n1))
    for g in range(FFT_R // FFT_GRP):
        n2 = slice(g * FFT_GRP, (g + 1) * FFT_GRP)
        xt = jnp.swapaxes(x_ref[...].reshape(n1, FFT_R, cb)[:, n2, :], 0, 1).astype(jnp.bfloat16)
        a = lax.dot_general(fb, xt, (((2,), (1,)), ((0,), (0,))), preferred_element_type=jnp.float32)
        o_ref[:, n2, :] = jnp.swapaxes(a, 0, 1).astype(o_ref.dtype)


def fft_first(x, rows, row_blk, col0, width, first, cb=128):
    n1 = rows // FFT_R
    return pl.pallas_call(
        _fft_first_kernel,
        grid=(width // cb,),
        in_specs=[pl.BlockSpec((rows, cb), lambda j: (row_blk, col0 // cb + j)),
                  pl.BlockSpec((2 * FFT_R, n1), lambda j: (0, 0))],
        out_specs=pl.BlockSpec((2 * FFT_R, FFT_R, cb), lambda j: (0, 0, j)),
        out_shape=jax.ShapeDtypeStruct((2 * FFT_R, FFT_R, width), jnp.bfloat16),
        compiler_params=pltpu.CompilerParams(
            dimension_semantics=("parallel",), vmem_limit_bytes=VMEM_LIMIT_BYTES),
    )(x, first[:, :n1].astype(jnp.bfloat16))


def _fft_last_kernel(q_ref, f_ref, z_ref, x_ref, b_ref, o_ref):
    kk, _, cb = q_ref.shape
    n1 = f_ref.shape[0]
    fb = jnp.broadcast_to(f_ref[...][None], (FFT_GRP, n1, kk))
    for g in range(FFT_R // FFT_GRP):
        qt = jnp.swapaxes(q_ref[:, g * FFT_GRP:(g + 1) * FFT_GRP, :].astype(jnp.float32), 0, 1)
        y = lax.dot_general(fb, qt.astype(jnp.bfloat16), (((2,), (1,)), ((0,), (0,))),
                            preferred_element_type=jnp.float32)
        yt = jnp.swapaxes(y, 0, 1)
        for i in range(n1):
            rows = slice(i * FFT_R + g * FFT_GRP, i * FFT_R + (g + 1) * FFT_GRP)
            o_ref[rows, :] = (x_ref[rows, :] * (yt[i] + z_ref[rows, :] * b_ref[...])).astype(o_ref.dtype)


def fft_last(q, last, z, z_col0, xg, xg_col0, bias, out_dtype, cb=128):
    C = q.shape[-1]
    L = z.shape[0]
    n1 = L // FFT_R
    return pl.pallas_call(
        _fft_last_kernel,
        grid=(C // cb,),
        in_specs=[pl.BlockSpec((2 * FFT_R, FFT_R, cb), lambda j: (0, 0, j)),
                  pl.BlockSpec((n1, 2 * FFT_R), lambda j: (0, 0)),
                  pl.BlockSpec((L, cb), lambda j: (0, z_col0 // cb + j)),
                  pl.BlockSpec((L, cb), lambda j: (0, xg_col0 // cb + j)),
                  pl.BlockSpec((1, cb), lambda j: (0, j))],
        out_specs=pl.BlockSpec((L, cb), lambda j: (0, j)),
        out_shape=jax.ShapeDtypeStruct((L, C), out_dtype),
        compiler_params=pltpu.CompilerParams(
            dimension_semantics=("parallel",), vmem_limit_bytes=VMEM_LIMIT_BYTES),
    )(q, last[:n1].astype(jnp.bfloat16), z, xg, bias)


def _hyena_filter_kernel(ft_ref, tn_ref, w1_ref, b1_ref, w2_ref, b2_ref, fr_ref, w3_ref, dl_ref, o_ref,
                         hdn_ref, win_ref):
    cols = o_ref.shape[-1]

    @pl.when(pl.program_id(1) == 0)
    def _():
        pre = jnp.dot(ft_ref[...], w1_ref[...], preferred_element_type=jnp.float32, precision=HIGHEST)
        hdn = jnp.sin(fr_ref[...] * (pre + b1_ref[...]))
        pre = jnp.dot(hdn, w2_ref[...], preferred_element_type=jnp.float32, precision=HIGHEST)
        hdn_ref[...] = jnp.sin(fr_ref[...] * (pre + b2_ref[...])).astype(hdn_ref.dtype)
        tn = tn_ref[...]
        for lt in range(cols // LANES):
            ch = slice(lt * LANES, (lt + 1) * LANES)
            win_ref[:, ch] = jnp.exp(-tn * dl_ref[:, ch]) + HY_SHIFT

    filt = jnp.dot(hdn_ref[...], w3_ref[0].astype(jnp.bfloat16), preferred_element_type=jnp.float32)
    o_ref[0] = filt * win_ref[...]


def hyena_filter_taps(L, w1, b1, w2, b2, w3, freq, tr=512):
    assert FFT_N == 2 * L
    n = np.arange(FFT_N)
    t = np.where(n < L, n, (FFT_N - n) % L).astype(np.float32)
    tnorm = t / np.float32(L)
    bands = np.linspace(1e-4, HY_BANDS - 1, HY_BANDS, dtype=np.float32)
    ang = (np.float32(2.0 * math.pi / L) * t[:, None] * bands[None, :]).astype(np.float64)
    feats = np.concatenate([tnorm[:, None], np.cos(ang), -np.sin(ang)], axis=-1)
    feats = np.pad(feats, ((0, 0), (0, HY_FEAT_PAD - feats.shape[1]))).astype(np.float32)
    tmark = np.broadcast_to(tnorm[:, None], (FFT_N, LANES))
    deltas = np.abs(np.linspace(HY_MIN_DECAY, HY_MAX_DECAY, HY_W, dtype=np.float32))[None, :]
    w1p = jnp.pad(w1, ((0, HY_FEAT_PAD - w1.shape[0]), (0, 0)))
    ffn = w2.shape[0]
    w3r = w3.reshape(ffn, 2 * HY_ORDER, HY_W).transpose(1, 0, 2)
    half = L // tr
    row = lambda r, o: (r, 0)
    fix = lambda r, o: (0, 0)
    return pl.pallas_call(
        _hyena_filter_kernel,
        grid=(FFT_N // tr, HY_ORDER),
        in_specs=[pl.BlockSpec((tr, HY_FEAT_PAD), row),
                  pl.BlockSpec((tr, LANES), row),
                  pl.BlockSpec((HY_FEAT_PAD, ffn), fix),
                  pl.BlockSpec((1, ffn), fix),
                  pl.BlockSpec((ffn, ffn), fix),
                  pl.BlockSpec((1, ffn), fix),
                  pl.BlockSpec((1, ffn), fix),
                  pl.BlockSpec((1, ffn, HY_W), lambda r, o: ((r // half) * HY_ORDER + o, 0, 0)),
                  pl.BlockSpec((1, HY_W), fix)],
        out_specs=pl.BlockSpec((1, tr, HY_W), lambda r, o: (o, r, 0)),
        out_shape=jax.ShapeDtypeStruct((HY_ORDER, FFT_N, HY_W), jnp.float32),
        scratch_shapes=[pltpu.VMEM((tr, ffn), jnp.bfloat16), pltpu.VMEM((tr, HY_W), jnp.float32)],
        compiler_params=pltpu.CompilerParams(
            dimension_semantics=("parallel", "arbitrary"), vmem_limit_bytes=VMEM_LIMIT_BYTES),
    )(jnp.asarray(feats), jnp.asarray(tmark), w1p, b1[None], w2, b2[None], freq[None], w3r, jnp.asarray(deltas))


def hyena(xs, taps, bias):
    L = xs.shape[0]
    C = xs.shape[1] // 3
    first, mid_fwd, mid_inv, last, tcos, tsin = _dft_tables()
    tc = jnp.broadcast_to(tcos[:, :, None], (FFT_R, FFT_R, LANES))
    ts = jnp.broadcast_to(tsin[:, :, None], (FFT_R, FFT_R, LANES))
    shape4 = (2, FFT_R, FFT_R, C)
    bf = jnp.bfloat16
    taps2 = taps.reshape(HY_ORDER * FFT_N, C)
    z, z_col0 = xs, 2 * C
    for o in range(HY_ORDER):
        ga = fft_first(taps2, FFT_N, o, 0, C, first)
        g = fft_mid(ga.reshape(shape4), tc, ts, mid_fwd, mid_inv, out_dtype=bf)
        a = fft_first(z, L, 0, z_col0, C, first)
        q = fft_mid(a.reshape(shape4), tc, ts, mid_fwd, mid_inv, g=g, out_dtype=bf)
        bias_eff = (bias[o] + taps[o, L])[None, :]
        z = fft_last(q.reshape(2 * FFT_R, FFT_R, C), last, z, z_col0, xs, o * C, bias_eff,
                     jnp.float32 if o + 1 < HY_ORDER else bf)
        z_col0 = 0
    return z


_CAND_ROWS = tuple(PEER_TOPK // (i + 1) for i in range(PEER_TOPK))


def _extract_top(s, rounds):
    rows = s.shape[0]
    iota = lax.broadcasted_iota(jnp.int32, s.shape, 0).astype(jnp.float32)
    tops = []
    for _ in range(rounds):
        m = jnp.max(s, axis=0, keepdims=True)
        first = jnp.min(jnp.where(s == m, iota, float(rows)), axis=0, keepdims=True)
        s = jnp.where(iota == first, NEG_INF, s)
        tops.append(m)
    return tops


def _peer_topk_kernel(q_ref, k_ref, s1_ref, s2_ref, e1_ref, e2_ref, thr_ref):
    tk = q_ref.shape[0]
    for u in range(tk // LANES):
        tok = slice(u * LANES, (u + 1) * LANES)
        s = []
        tops = []
        for p in range(2):
            qp = q_ref[tok, p * PEER_NKEYS:(p + 1) * PEER_NKEYS]
            sp = lax.dot_general(k_ref[0, p], qp, (((1,), (1,)), ((), ())),
                                 preferred_element_type=jnp.float32)
            s.append(sp)
            tops.append(_extract_top(sp, PEER_TOPK))
        top2a = jnp.concatenate(tops[1][:8], axis=0)
        top2b = jnp.concatenate(tops[1][8:], axis=0)
        row = lax.broadcasted_iota(jnp.int32, (8, LANES), 0)
        cands = [tops[0][0] + top2a, tops[0][0] + top2b]
        for i in range(1, PEER_TOPK):
            cands.append(jnp.where(row < _CAND_ROWS[i], tops[0][i] + top2a, NEG_INF))
        best = _extract_top(jnp.concatenate(cands, axis=0), PEER_TOPK)
        z = jnp.ones_like(best[0])
        for b in best[1:]:
            z = z + jnp.exp(b - best[0])
        s1_ref[0, :, tok] = s[0]
        s2_ref[0, :, tok] = s[1]
        e1_ref[0, :, tok] = jnp.exp(s[0] - tops[0][0]) / z
        e2_ref[0, :, tok] = jnp.exp(s[1] - tops[1][0])
        thr_ref[0, :, tok] = best[-1]


def peer_topk(q, sub_keys, tk=256):
    T = q.shape[0]
    H = PEER_HEADS
    big = jax.ShapeDtypeStruct((H, PEER_NKEYS, T), jnp.float32)
    big_spec = pl.BlockSpec((1, PEER_NKEYS, tk), lambda i, h: (h, 0, i))
    s1, s2, e1, e2, thr = pl.pallas_call(
        _peer_topk_kernel,
        grid=(T // tk, H),
        in_specs=[pl.BlockSpec((tk, PEER_DKEY), lambda i, h: (i, h)),
                  pl.BlockSpec((1, 2, PEER_NKEYS, PEER_DKEY // 2), lambda i, h: (h, 0, 0, 0))],
        out_specs=[big_spec, big_spec, big_spec, big_spec,
                   pl.BlockSpec((1, 1, tk), lambda i, h: (h, 0, i))],
        out_shape=[big, big, big, big, jax.ShapeDtypeStruct((H, 1, T), jnp.float32)],
        compiler_params=pltpu.CompilerParams(
            dimension_semantics=("parallel", "parallel"),
            vmem_limit_bytes=VMEM_LIMIT_BYTES),
    )(q, sub_keys.astype(jnp.bfloat16))
    return s1, s2, e1, e2, thr.reshape(H, T)


W_ROWS = 16


def _peer_expert_kernel(v_ref, u_ref, vt_ref, s1a_ref, e1a_ref, s2_ref, e2_ref, thr_ref, o_ref, sc_ref, w_ref):
    j = pl.program_id(1)
    tt = v_ref.shape[0]
    na = u_ref.shape[0] // PEER_NKEYS

    def scores():
        sc_ref[...] = lax.dot_general(u_ref[...], v_ref[...], (((1,), (1,)), ((), ())),
                                      preferred_element_type=jnp.float32)

    def weights(al, w_ref):
        for u in range(tt // LANES):
            tok = slice(u * LANES, (u + 1) * LANES)
            for part in range(PEER_NKEYS // W_ROWS):
                b = slice(part * W_ROWS, (part + 1) * W_ROWS)
                g = jnp.zeros((W_ROWS, LANES), jnp.float32)
                for h in range(PEER_HEADS):
                    s1row = s1a_ref[al, h:h + 1, tok]
                    e1row = e1a_ref[al, h:h + 1, tok]
                    keep = (s1row + s2_ref[h, b, tok]) >= thr_ref[h:h + 1, tok]
                    g = g + jnp.where(keep, e1row * e2_ref[h, b, tok], 0.0)
                rows = pl.ds(pl.multiple_of(al * PEER_NKEYS + part * W_ROWS, W_ROWS), W_ROWS)
                sc = sc_ref[rows, tok]
                act = 0.5 * sc * (1.0 + lax.erf(sc * math.sqrt(0.5)))
                w_ref[rows, tok] = (act * g).astype(w_ref.dtype)

    @pl.when(j == 0)
    def _():
        o_ref[...] = jnp.zeros_like(o_ref)

    scores()

    def body(al, carry):
        weights(al, w_ref)
        return carry
    lax.fori_loop(0, na, body, 0)
    o_ref[...] += jnp.dot(vt_ref[...], w_ref[...], preferred_element_type=jnp.float32)


def peer_experts(v, expert_u, expert_vt, s1a, e1a, s2, e2, thr, tt=512, nb=1024):
    T, D = v.shape
    N = expert_u.shape[0]
    na = nb // PEER_NKEYS
    H = PEER_HEADS
    return pl.pallas_call(
        _peer_expert_kernel,
        grid=(T // tt, N // nb),
        in_specs=[pl.BlockSpec((tt, D), lambda i, j: (i, 0)),
                  pl.BlockSpec((nb, D), lambda i, j: (j, 0)),
                  pl.BlockSpec((D, nb), lambda i, j: (0, j)),
                  pl.BlockSpec((na, H, tt), lambda i, j: (j, 0, i)),
                  pl.BlockSpec((na, H, tt), lambda i, j: (j, 0, i)),
                  pl.BlockSpec((H, PEER_NKEYS, tt), lambda i, j: (0, 0, i)),
                  pl.BlockSpec((H, PEER_NKEYS, tt), lambda i, j: (0, 0, i)),
                  pl.BlockSpec((H, tt), lambda i, j: (0, i))],
        out_specs=pl.BlockSpec((D, tt), lambda i, j: (0, i)),
        out_shape=jax.ShapeDtypeStruct((D, T), jnp.float32),
        scratch_shapes=[pltpu.VMEM((nb, tt), jnp.float32), pltpu.VMEM((nb, tt), jnp.bfloat16)],
        compiler_params=pltpu.CompilerParams(
            dimension_semantics=("parallel", "arbitrary"),
            vmem_limit_bytes=VMEM_LIMIT_BYTES),
    )(v, expert_u, expert_vt, s1a, e1a, s2, e2, thr)


def peer(u, w_q, sub_keys, expert_u, expert_v, tk=256, tt=512, nb=1024):
    ub = u.astype(jnp.bfloat16)
    q = pmm(ub, w_q.astype(jnp.bfloat16), out_dtype=jnp.bfloat16)
    s1, s2, e1, e2, thr = peer_topk(q, sub_keys, tk=tk)
    out_t = peer_experts(ub, expert_u.astype(jnp.bfloat16), expert_v.astype(jnp.bfloat16).T,
                         s1.transpose(1, 0, 2), e1.transpose(1, 0, 2), s2, e2, thr, tt=tt, nb=nb)
    return out_t.T


ROW_BLK = 256
STATE_MIN = -1e30

COL_QK, COL_V, COL_O = 0, 2 * ML_QK_W, 2 * ML_QK_W + ML_V_W
COL_HY = COL_O + ML_V_W
COL_BG = COL_HY + 3 * HY_W
COL_GATE = COL_BG + 2 * D_MODEL
IN_W_PAD = 16896


def _params(*sem):
    return pltpu.CompilerParams(dimension_semantics=sem, vmem_limit_bytes=VMEM_LIMIT_BYTES)


def _norm_mod_kernel(x_ref, c_ref, g_ref, sc_ref, sh_ref, o_ref, *, n_lat_blk):
    def emit(x):
        y = x * lax.rsqrt(jnp.mean(x * x, axis=-1, keepdims=True) + EPS)
        o_ref[...] = (y * g_ref[...] * (1.0 + sc_ref[0]) + sh_ref[0]).astype(o_ref.dtype)

    @pl.when(pl.program_id(0) < n_lat_blk)
    def _():
        emit(x_ref[...])

    @pl.when(pl.program_id(0) >= n_lat_blk)
    def _():
        emit(c_ref[...])


def norm_mod(x, ctx, g, sc, sh):
    L, D = x.shape
    n_lat_blk = L // ROW_BLK
    mod = pl.BlockSpec((1, 1, D), lambda i: (i // n_lat_blk, 0, 0))
    return pl.pallas_call(
        functools.partial(_norm_mod_kernel, n_lat_blk=n_lat_blk),
        grid=(n_lat_blk + 1,),
        in_specs=[pl.BlockSpec((ROW_BLK, D), lambda i: (jnp.minimum(i, n_lat_blk - 1), 0)),
                  pl.BlockSpec((ROW_BLK, D), lambda i: (0, 0)),
                  pl.BlockSpec((1, D), lambda i: (0, 0)), mod, mod],
        out_specs=pl.BlockSpec((ROW_BLK, D), lambda i: (i, 0)),
        out_shape=jax.ShapeDtypeStruct((L + ROW_BLK, D), jnp.bfloat16),
        compiler_params=_params("parallel"),
    )(x, ctx, g[None], sc, sh)


def _fill_halo(buf_ref, prev_ref, cur_ref, next_ref):
    r = cur_ref.shape[0]
    h = prev_ref.shape[0]
    buf_ref[0:h, :] = prev_ref[...]
    buf_ref[h:h + r, :] = cur_ref[...]
    buf_ref[h + r:2 * h + r, :] = next_ref[...]
    return h


def _qk_conv_kernel(prev_ref, cur_ref, next_ref, w_ref, o_ref, buf_ref, *, n_lat_blk, q_slabs):
    j = pl.program_id(0)
    i = pl.program_id(1)
    r, cw = cur_ref.shape
    halo = _fill_halo(buf_ref, prev_ref, cur_ref, next_ref)
    local = lax.broadcasted_iota(jnp.int32, (r, cw), 0)
    n_lat = n_lat_blk * r

    def tap(off):
        return buf_ref[pl.ds(halo + off, r), :]

    def finish(acc):
        y = acc * jax.nn.sigmoid(acc)
        scale = jnp.where(j < q_slabs, ML_DQK ** -0.5, 1.0)
        o_ref[...] = (y * scale).astype(o_ref.dtype)

    @pl.when(i < n_lat_blk)
    def _():
        g = i * r + local
        col = jnp.bitwise_and(g, GRID_W - 1)
        acc = jnp.zeros((r, cw), jnp.float32)
        for dr in (-1, 0, 1):
            for dc in (-1, 0, 1):
                t = tap(GRID_W * dr + dc)
                if dr == -1:
                    t = jnp.where(g >= GRID_W, t, 0.0)
                if dr == 1:
                    t = jnp.where(g < n_lat - GRID_W, t, 0.0)
                if dc == -1:
                    t = jnp.where(col >= 1, t, 0.0)
                if dc == 1:
                    t = jnp.where(col < GRID_W - 1, t, 0.0)
                acc = acc + t * w_ref[3 * (dr + 1) + dc + 1:3 * (dr + 1) + dc + 2, :]
        finish(acc)

    @pl.when(i >= n_lat_blk)
    def _():
        acc = tap(0) * w_ref[4:5, :]
        acc = acc + jnp.where(local >= 1, tap(-1), 0.0) * w_ref[3:4, :]
        acc = acc + jnp.where(local < r - 1, tap(1), 0.0) * w_ref[5:6, :]
        finish(acc)


def _halo_specs(cw, col0_blk, nblk, halo):
    per = ROW_BLK // halo
    cur = lambda j, i: (i, col0_blk + j)
    prev = lambda j, i: (jnp.maximum(i * per - 1, 0), col0_blk + j)
    nxt = lambda j, i: (jnp.minimum((i + 1) * per, nblk * per - 1), col0_blk + j)
    return [pl.BlockSpec((halo, cw), prev), pl.BlockSpec((ROW_BLK, cw), cur), pl.BlockSpec((halo, cw), nxt)]


QK_HALO = 128
SEQ_HALO = 8


def qk_conv(p, w9, n_lat_blk, cw=256):
    T = p.shape[0]
    nblk = T // ROW_BLK
    width = 2 * ML_QK_W
    return pl.pallas_call(
        functools.partial(_qk_conv_kernel, n_lat_blk=n_lat_blk, q_slabs=ML_QK_W // cw),
        grid=(width // cw, nblk),
        in_specs=_halo_specs(cw, COL_QK // cw, nblk, QK_HALO) + [pl.BlockSpec((9, cw), lambda j, i: (0, j))],
        out_specs=pl.BlockSpec((ROW_BLK, cw), lambda j, i: (i, j)),
        out_shape=jax.ShapeDtypeStruct((T, width), jnp.bfloat16),
        scratch_shapes=[pltpu.VMEM((ROW_BLK + 2 * QK_HALO, cw), jnp.float32)],
        compiler_params=_params("parallel", "parallel"),
    )(p, p, p, w9)


def _seq_conv_kernel(prev_ref, cur_ref, next_ref, w_ref, o_ref, buf_ref, *, n_rows):
    i = pl.program_id(1)
    r, cw = cur_ref.shape
    halo = _fill_halo(buf_ref, prev_ref, cur_ref, next_ref)
    g = i * r + lax.broadcasted_iota(jnp.int32, (r, cw), 0)
    acc = buf_ref[pl.ds(halo, r), :] * w_ref[1:2, :]
    acc = acc + jnp.where(g >= 1, buf_ref[pl.ds(halo - 1, r), :], 0.0) * w_ref[0:1, :]
    acc = acc + jnp.where(g < n_rows - 1, buf_ref[pl.ds(halo + 1, r), :], 0.0) * w_ref[2:3, :]
    o_ref[...] = acc.astype(o_ref.dtype)


def hy_conv(p, w3, n_lat_blk, cw=512):
    nblk = p.shape[0] // ROW_BLK
    width = 3 * HY_W
    n_rows = n_lat_blk * ROW_BLK
    return pl.pallas_call(
        functools.partial(_seq_conv_kernel, n_rows=n_rows),
        grid=(width // cw, n_lat_blk),
        in_specs=_halo_specs(cw, COL_HY // cw, nblk, SEQ_HALO) + [pl.BlockSpec((3, cw), lambda j, i: (0, j))],
        out_specs=pl.BlockSpec((ROW_BLK, cw), lambda j, i: (i, j)),
        out_shape=jax.ShapeDtypeStruct((n_rows, width), jnp.float32),
        scratch_shapes=[pltpu.VMEM((ROW_BLK + 2 * SEQ_HALO, cw), jnp.float32)],
        compiler_params=_params("parallel", "parallel"),
    )(p, p, p, w3)


def _mlstm_kernel(q_ref, k_ref, kt_ref, v_ref, g_ref, o_ref, c_ref, n_ref, m_ref):
    d = pl.program_id(0)
    step = pl.program_id(2)
    r = q_ref.shape[0]

    @pl.when(step == 0)
    def _():
        c_ref[...] = jnp.zeros_like(c_ref)
        n_ref[...] = jnp.zeros_like(n_ref)
        m_ref[...] = jnp.full_like(m_ref, STATE_MIN)

    ig = g_ref[0, 0, 0:1, :]
    fg = g_ref[0, 0, 1:2, :]
    t_i = lax.broadcasted_iota(jnp.int32, (r, r), 0)
    s_i = lax.broadcasted_iota(jnp.int32, (r, r), 1)
    sgn = jnp.where(d == 0, 1, -1)
    reads = (s_i - t_i) * sgn <= 0
    read_by = (t_i - s_i) * sgn <= 0
    eye = s_i == t_i
    f_col = jnp.sum(jnp.where(eye, fg, 0.0), axis=1, keepdims=True)
    cum_col = jnp.sum(jnp.where(reads, fg, 0.0), axis=1, keepdims=True)
    cum_row = jnp.sum(jnp.where(read_by, f_col, 0.0), axis=0, keepdims=True)
    total = jnp.sum(fg, axis=1, keepdims=True)
    m_old = m_ref[...]

    dmat = jnp.where(reads, (cum_col - cum_row) + ig, NEG_INF)
    inter = cum_col + m_old
    m_t = jnp.maximum(inter, jnp.max(dmat, axis=1, keepdims=True))
    w = jnp.exp(dmat - m_t)
    s_inter = jnp.exp(inter - m_t)
    q = q_ref[...]
    v = v_ref[...]
    qkw = jnp.dot(q, kt_ref[...], preferred_element_type=jnp.float32) * w
    qn = jnp.sum(q.astype(jnp.float32) * n_ref[...], axis=1, keepdims=True)
    den = jnp.sum(qkw, axis=1, keepdims=True) + s_inter * qn
    num = jnp.dot(qkw.astype(jnp.bfloat16), v.astype(jnp.bfloat16), preferred_element_type=jnp.float32) \
        + s_inter * jnp.dot(q, c_ref[...].astype(jnp.bfloat16), preferred_element_type=jnp.float32)
    o_ref[0] = num / jnp.maximum(jnp.abs(den), jnp.exp(-m_t))

    dec = (total - cum_row) + ig
    m_new = jnp.maximum(total + m_old, jnp.max(dec, axis=1, keepdims=True))
    wk_row = jnp.exp(dec - m_new)
    s_old = jnp.exp(total + m_old - m_new)
    wk_col = jnp.sum(jnp.where(eye, wk_row, 0.0), axis=1, keepdims=True)
    c_ref[...] = s_old * c_ref[...] + jnp.dot(kt_ref[...], (wk_col * v).astype(jnp.bfloat16),
                                              preferred_element_type=jnp.float32)
    n_ref[...] = s_old * n_ref[...] + jnp.sum(wk_col * k_ref[...].astype(jnp.float32), axis=0, keepdims=True)
    m_ref[...] = m_new


def mlstm_scan(qk, kt, p, gates, n_lat_blk):
    T = qk.shape[0]
    r = ROW_BLK

    def blk(d, s):
        lat = jnp.where(d == 0, s - 1, n_lat_blk - s)
        return jnp.where(s == 0, n_lat_blk, lat)

    return pl.pallas_call(
        _mlstm_kernel,
        grid=(2, ML_HEADS, n_lat_blk + 1),
        in_specs=[pl.BlockSpec((r, ML_DQK), lambda d, h, s: (blk(d, s), h)),
                  pl.BlockSpec((r, ML_DQK), lambda d, h, s: (blk(d, s), ML_HEADS + h)),
                  pl.BlockSpec((ML_DQK, r), lambda d, h, s: (h, blk(d, s))),
                  pl.BlockSpec((r, ML_DV), lambda d, h, s: (blk(d, s), COL_V // ML_DV + h)),
                  pl.BlockSpec((1, 1, 2, r), lambda d, h, s: (d, h, 0, blk(d, s)))],
        out_specs=pl.BlockSpec((1, r, ML_DV), lambda d, h, s: (d, blk(d, s), h)),
        out_shape=jax.ShapeDtypeStruct((2, T, ML_V_W), jnp.float32),
        scratch_shapes=[pltpu.VMEM((ML_DQK, ML_DV), jnp.float32), pltpu.VMEM((1, ML_DQK), jnp.float32),
                        pltpu.VMEM((1, 1), jnp.float32)],
        compiler_params=_params("parallel", "parallel", "arbitrary"),
    )(qk, qk, kt, p, gates)


def _ml_out_kernel(hf_ref, hb_ref, o_ref, g_ref, out_ref):
    for h in range(ML_HEADS):
        ch = slice(h * ML_DV, (h + 1) * ML_DV)
        hh = hf_ref[0, :, ch] + hb_ref[0, :, ch]
        hh = hh * lax.rsqrt(jnp.mean(hh * hh, axis=-1, keepdims=True) + EPS)
        out_ref[:, ch] = (hh * g_ref[:, ch] * jax.nn.sigmoid(o_ref[:, ch])).astype(out_ref.dtype)


def ml_out(h2, p, gain, n_lat_blk):
    r = ROW_BLK
    return pl.pallas_call(
        _ml_out_kernel,
        grid=(n_lat_blk,),
        in_specs=[pl.BlockSpec((1, r, ML_V_W), lambda i: (0, i, 0)),
                  pl.BlockSpec((1, r, ML_V_W), lambda i: (1, i, 0)),
                  pl.BlockSpec((r, ML_V_W), lambda i: (i, COL_O // ML_V_W)),
                  pl.BlockSpec((1, ML_V_W), lambda i: (0, 0))],
        out_specs=pl.BlockSpec((r, ML_V_W), lambda i: (i, 0)),
        out_shape=jax.ShapeDtypeStruct((n_lat_blk * r, ML_V_W), jnp.bfloat16),
        compiler_params=_params("parallel"),
    )(h2, h2, p, gain[None])


def _merge_kernel(hm_ref, hh_ref, wm_ref, wh_ref, gm_ref, gh_ref, o_ref):
    ym = jnp.dot(hm_ref[...], wm_ref[...], preferred_element_type=jnp.float32)
    yh = jnp.dot(hh_ref[...], wh_ref[...], preferred_element_type=jnp.float32)
    o_ref[...] = (jax.nn.sigmoid(gm_ref[...]) * ym + jax.nn.sigmoid(gh_ref[...]) * yh).astype(o_ref.dtype)


def merge_branches(h_ml, h_hy, w_pm, w_ph, p, tm=512, tn=1024):
    L, K = h_ml.shape
    D = w_pm.shape[1]
    act = pl.BlockSpec((tm, K), lambda j, i: (i, 0))
    wgt = pl.BlockSpec((K, tn), lambda j, i: (0, j))
    return pl.pallas_call(
        _merge_kernel,
        grid=(D // tn, L // tm),
        in_specs=[act, act, wgt, wgt,
                  pl.BlockSpec((tm, tn), lambda j, i: (i, COL_BG // tn + j)),
                  pl.BlockSpec((tm, tn), lambda j, i: (i, (COL_BG + D) // tn + j))],
        out_specs=pl.BlockSpec((tm, tn), lambda j, i: (i, j)),
        out_shape=jax.ShapeDtypeStruct((L, D), jnp.bfloat16),
        compiler_params=_params("parallel", "parallel"),
    )(h_ml, h_hy, w_pm, w_ph, p, p)


def _out_proj_kernel(y_ref, w_ref, x_ref, g1_ref, ng_ref, sc_ref, sh_ref, h_ref, v_ref):
    h = x_ref[...] + g1_ref[...] * jnp.dot(y_ref[...], w_ref[...], preferred_element_type=jnp.float32)
    h_ref[...] = h
    n = h * lax.rsqrt(jnp.mean(h * h, axis=-1, keepdims=True) + EPS)
    v_ref[...] = (n * ng_ref[...] * (1.0 + sc_ref[...]) + sh_ref[...]).astype(v_ref.dtype)


def out_proj(y, w_o, x, g1, ng, sc, sh, tm=256):
    L, D = x.shape
    row = pl.BlockSpec((tm, D), lambda i: (i, 0))
    vec = pl.BlockSpec((1, D), lambda i: (0, 0))
    return pl.pallas_call(
        _out_proj_kernel,
        grid=(L // tm,),
        in_specs=[row, pl.BlockSpec((D, D), lambda i: (0, 0)), row, vec, vec, vec, vec],
        out_specs=[row, row],
        out_shape=[jax.ShapeDtypeStruct((L, D), jnp.float32), jax.ShapeDtypeStruct((L, D), jnp.bfloat16)],
        compiler_params=_params("parallel"),
    )(y, w_o, x, g1, ng, sc, sh)


def _final_kernel(h_ref, p_ref, g2_ref, fg_ref, o_ref):
    h = h_ref[...] + g2_ref[...] * p_ref[...]
    o_ref[...] = h * lax.rsqrt(jnp.mean(h * h, axis=-1, keepdims=True) + EPS) * fg_ref[...]


def final_norm(h, peer_out, g2, fg, tm=256):
    L, D = h.shape
    row = pl.BlockSpec((tm, D), lambda i: (i, 0))
    vec = pl.BlockSpec((1, D), lambda i: (0, 0))
    return pl.pallas_call(
        _final_kernel,
        grid=(L // tm,),
        in_specs=[row, row, vec, vec],
        out_specs=row,
        out_shape=jax.ShapeDtypeStruct((L, D), jnp.float32),
        compiler_params=_params("parallel"),
    )(h, peer_out, g2, fg)


def token_mixer_inputs(x, ctx, mods, norm_g, w_in):
    L = x.shape[0]
    assert ctx.shape[0] == ROW_BLK and L % ROW_BLK == 0
    n_lat_blk = L // ROW_BLK
    sh1, sc1 = mods[:, 0], mods[:, 1]
    u = norm_mod(x, ctx, norm_g, sc1[:, None, :], sh1[:, None, :])
    n_in = w_in.shape[1]
    w_r = jnp.concatenate([w_in[:, :COL_O + ML_V_W], w_in[:, COL_O + ML_V_W + 4 * ML_HEADS:],
                           w_in[:, COL_O + ML_V_W:COL_O + ML_V_W + 4 * ML_HEADS],
                           jnp.zeros((w_in.shape[0], IN_W_PAD - n_in), w_in.dtype)], axis=1).astype(jnp.bfloat16)
    return pmm(u, w_r, tm=768, tn=1536), n_lat_blk


def kernel(x, c, ctx, c_ctx, w_mod, b_mod, norm1_g, norm2_g, final_g, w_in, ml_conv_w, ml_gate_b, ml_norm_g, hy_conv_w, hy_w1, hy_b1, hy_w2, hy_b2, hy_w3, hy_freq, hy_bias, w_proj_ml, w_proj_hy, w_out, peer_wq, peer_keys, peer_u, peer_v):
    l = 0
    L, D = x.shape[1:]
    bf = jnp.bfloat16
    s_rows = jnp.zeros((8, D), jnp.float32).at[0].set(jax.nn.silu(c[0])).at[1].set(jax.nn.silu(c_ctx))
    mods = (pmm(s_rows, w_mod[l], tm=8, tn=1024)[:2] + b_mod[l]).reshape(2, 6, D)

    p, n_lat_blk = token_mixer_inputs(x[0], ctx[0], mods, norm1_g[l], w_in[l])
    T = p.shape[0]

    qk = qk_conv(p, ml_conv_w[l].reshape(9, 2 * ML_QK_W), n_lat_blk)
    kt = qk[:, ML_QK_W:].T
    gt = p[:, COL_GATE:COL_GATE + 4 * ML_HEADS].reshape(T, 4, ML_HEADS) + ml_gate_b[l]
    gates = jnp.stack([jnp.stack([gt[:, 0], jax.nn.log_sigmoid(gt[:, 1])]),
                       jnp.stack([gt[:, 2], jax.nn.log_sigmoid(gt[:, 3])])]).transpose(0, 3, 1, 2)
    h2 = mlstm_scan(qk, kt, p, gates, n_lat_blk)
    h_ml = ml_out(h2, p, ml_norm_g[l], n_lat_blk)

    xs = hy_conv(p, hy_conv_w[l], n_lat_blk)
    taps = hyena_filter_taps(L, hy_w1[l], hy_b1[l], hy_w2[l], hy_b2[l], hy_w3[l], hy_freq[l])
    h_hy = hyena(xs, taps, hy_bias[l])

    y = merge_branches(h_ml, h_hy.astype(bf), w_proj_ml[l].astype(bf), w_proj_hy[l].astype(bf), p)
    h_lat, v_lat = out_proj(y, w_out[l].astype(bf), x[0], mods[0:1, 2], norm2_g[l][None], mods[0:1, 4], mods[0:1, 3])
    peer_out = peer(v_lat, peer_wq[l], peer_keys[l], peer_u[l], peer_v[l])
    return final_norm(h_lat, peer_out, mods[0:1, 5], final_g[None])[None]
```

```python
import functools
import math
import numpy as np
import jax
import jax.numpy as jnp
from jax import lax
from jax.experimental import pallas as pl
from jax.experimental.pallas import tpu as pltpu

D_MODEL = 2048
GRID_W = 64
EPS = 1e-6

ML_HEADS = 8
ML_DQK = D_MODEL // 16
ML_DV = D_MODEL // 8
ML_QK_W = ML_HEADS * ML_DQK
ML_V_W = ML_HEADS * ML_DV

HY_W = D_MODEL
HY_ORDER = 2
HY_BANDS = 8
HY_SHIFT = 0.05
HY_MIN_DECAY = math.log(1e-2) / 1.5
HY_MAX_DECAY = math.log(1e-2) / 0.3

PEER_HEADS = 8
PEER_NKEYS = 128
PEER_DKEY = 256
PEER_TOPK = 16

LANES = 128
NEG_INF = float('-inf')
VMEM_LIMIT_BYTES = 56 * 1024 * 1024


def _mm_kernel(a_ref, b_ref, o_ref):
    o_ref[...] = jnp.dot(a_ref[...].astype(jnp.bfloat16), b_ref[...].astype(jnp.bfloat16),
                         preferred_element_type=jnp.float32).astype(o_ref.dtype)


def pmm(a, b, tm=512, tn=512, out_dtype=jnp.float32):
    M, K = a.shape
    _, N = b.shape
    tm = min(tm, M)
    tn = min(tn, N)
    assert M % tm == 0 and N % tn == 0, (M, N, tm, tn)
    return pl.pallas_call(
        _mm_kernel,
        grid=(N // tn, M // tm),
        in_specs=[pl.BlockSpec((tm, K), lambda j, i: (i, 0)),
                  pl.BlockSpec((K, tn), lambda j, i: (0, j))],
        out_specs=pl.BlockSpec((tm, tn), lambda j, i: (i, j)),
        out_shape=jax.ShapeDtypeStruct((M, N), out_dtype),
        compiler_params=pltpu.CompilerParams(
            dimension_semantics=("parallel", "parallel"),
            vmem_limit_bytes=VMEM_LIMIT_BYTES),
    )(a, b)


FFT_R = 128
FFT_N = FFT_R * FFT_R
HY_FEAT_PAD = 32
HIGHEST = lax.Precision.HIGHEST


def _dft_tables():
    r = np.arange(FFT_R)
    ang = 2.0 * np.pi * np.outer(r, r) / FFT_R
    c, s = np.cos(ang), np.sin(ang)
    first = np.concatenate([c, -s], axis=0)
    mid_fwd = np.block([[c, s], [-s, c]])
    mid_inv = np.block([[c, -s], [s, c]])
    last = np.concatenate([c, -s], axis=1) / FFT_N
    tw = 2.0 * np.pi * np.outer(r, r) / FFT_N
    f32 = lambda a: jnp.asarray(a, jnp.float32)
    return f32(first), f32(mid_fwd), f32(mid_inv), f32(last), f32(np.cos(tw)), f32(np.sin(tw))


def _fft_mid_kernel(a_ref, tc_ref, ts_ref, mf_ref, mi_ref, *rest, inverse):
    if inverse:
        g_ref, o_ref, b_ref = rest
    else:
        o_ref, b_ref = rest
    cb = a_ref.shape[-1]
    tc = tc_ref[0]
    ts = ts_ref[0]
    for lt in range(cb // LANES):
        ch = slice(lt * LANES, (lt + 1) * LANES)
        ar = a_ref[0, 0, :, ch]
        ai = a_ref[1, 0, :, ch]
        b_ref[0:FFT_R, ch] = (ar * tc + ai * ts).astype(b_ref.dtype)
        b_ref[FFT_R:, ch] = (ai * tc - ar * ts).astype(b_ref.dtype)
    x = jnp.dot(mf_ref[...], b_ref[...], preferred_element_type=jnp.float32)
    if not inverse:
        o_ref[0, 0] = x[:FFT_R].astype(o_ref.dtype)
        o_ref[1, 0] = x[FFT_R:].astype(o_ref.dtype)
        return
    xr, xi = x[:FFT_R], x[FFT_R:]
    gr = g_ref[0, 0].astype(jnp.float32)
    gi = g_ref[1, 0].astype(jnp.float32)
    b_ref[0:FFT_R, :] = (xr * gr - xi * gi).astype(b_ref.dtype)
    b_ref[FFT_R:, :] = (xr * gi + xi * gr).astype(b_ref.dtype)
    q = jnp.dot(mi_ref[...], b_ref[...], preferred_element_type=jnp.float32)
    for lt in range(cb // LANES):
        ch = slice(lt * LANES, (lt + 1) * LANES)
        qr = q[:FFT_R, ch]
        qi = q[FFT_R:, ch]
        o_ref[0, 0, :, ch] = (qr * tc - qi * ts).astype(o_ref.dtype)
        o_ref[1, 0, :, ch] = (qi * tc + qr * ts).astype(o_ref.dtype)


def fft_mid(a, tc, ts, mid_fwd, mid_inv, g=None, cb=2048, out_dtype=jnp.float32):
    C = a.shape[-1]
    cb = min(cb, C)
    inverse = g is not None
    blk = pl.BlockSpec((2, 1, FFT_R, cb), lambda k, j: (0, k, 0, j))
    tw = pl.BlockSpec((1, FFT_R, LANES), lambda k, j: (k, 0, 0))
    mat = pl.BlockSpec((2 * FFT_R, 2 * FFT_R), lambda k, j: (0, 0))
    return pl.pallas_call(
        functools.partial(_fft_mid_kernel, inverse=inverse),
        grid=(FFT_R, C // cb),
        in_specs=[blk, tw, tw, mat, mat] + ([blk] if inverse else []),
        out_specs=blk,
        out_shape=jax.ShapeDtypeStruct(a.shape, out_dtype),
        scratch_shapes=[pltpu.VMEM((2 * FFT_R, cb), jnp.bfloat16)],
        compiler_params=pltpu.CompilerParams(
            dimension_semantics=("parallel", "parallel"),
            vmem_limit_bytes=VMEM_LIMIT_BYTES),
    )(a, tc, ts, mid_fwd.astype(jnp.bfloat16), mid_inv.astype(jnp.bfloat16), *([g] if inverse else []))


FFT_GRP = 32


def _fft_first_kernel(x_ref, f_ref, o_ref):
    rows, cb = x_ref.shape
    n1 = rows // FFT_R
    kk = f_ref.shape[0]
    fb = jnp.broadcast_to(f_ref[...][None], (FFT_GRP, kk, n1))
    for g in range(FFT_R // FFT_GRP):
        n2 = slice(g * FFT_GRP, (g + 1) * FFT_GRP)
        xt = jnp.swapaxes(x_ref[...].reshape(n1, FFT_R, cb)[:, n2, :], 0, 1).astype(jnp.bfloat16)
        a = lax.dot_general(fb, xt, (((2,), (1,)), ((0,), (0,))), preferred_element_type=jnp.float32)
        o_ref[:, n2, :] = jnp.swapaxes(a, 0, 1).astype(o_ref.dtype)


def fft_first(x, rows, row_blk, col0, width, first, cb=128):
    n1 = rows // FFT_R
    return pl.pallas_call(
        _fft_first_kernel,
        grid=(width // cb,),
        in_specs=[pl.BlockSpec((rows, cb), lambda j: (row_blk, col0 // cb + j)),
                  pl.BlockSpec((2 * FFT_R, n1), lambda j: (0, 0))],
        out_specs=pl.BlockSpec((2 * FFT_R, FFT_R, cb), lambda j: (0, 0, j)),
        out_shape=jax.ShapeDtypeStruct((2 * FFT_R, FFT_R, width), jnp.bfloat16),
        compiler_params=pltpu.CompilerParams(
            dimension_semantics=("parallel",), vmem_limit_bytes=VMEM_LIMIT_BYTES),
    )(x, first[:, :n1].astype(jnp.bfloat16))


def _fft_last_kernel(q_ref, f_ref, z_ref, x_ref, b_ref, o_ref):
    kk, _, cb = q_ref.shape
    n1 = f_ref.shape[0]
    fb = jnp.broadcast_to(f_ref[...][None], (FFT_GRP, n1, kk))
    for g in range(FFT_R // FFT_GRP):
        qt = jnp.swapaxes(q_ref[:, g * FFT_GRP:(g + 1) * FFT_GRP, :].astype(jnp.float32), 0, 1)
        y = lax.dot_general(fb, qt.astype(jnp.bfloat16), (((2,), (1,)), ((0,), (0,))),
                            preferred_element_type=jnp.float32)
        yt = jnp.swapaxes(y, 0, 1)
        for i in range(n1):
            rows = slice(i * FFT_R + g * FFT_GRP, i * FFT_R + (g + 1) * FFT_GRP)
            o_ref[rows, :] = (x_ref[rows, :] * (yt[i] + z_ref[rows, :] * b_ref[...])).astype(o_ref.dtype)


def fft_last(q, last, z, z_col0, xg, xg_col0, bias, out_dtype, cb=128):
    C = q.shape[-1]
    L = z.shape[0]
    n1 = L // FFT_R
    return pl.pallas_call(
        _fft_last_kernel,
        grid=(C // cb,),
        in_specs=[pl.BlockSpec((2 * FFT_R, FFT_R, cb), lambda j: (0, 0, j)),
                  pl.BlockSpec((n1, 2 * FFT_R), lambda j: (0, 0)),
                  pl.BlockSpec((L, cb), lambda j: (0, z_col0 // cb + j)),
                  pl.BlockSpec((L, cb), lambda j: (0, xg_col0 // cb + j)),
                  pl.BlockSpec((1, cb), lambda j: (0, j))],
        out_specs=pl.BlockSpec((L, cb), lambda j: (0, j)),
        out_shape=jax.ShapeDtypeStruct((L, C), out_dtype),
        compiler_params=pltpu.CompilerParams(
            dimension_semantics=("parallel",), vmem_limit_bytes=VMEM_LIMIT_BYTES),
    )(q, last[:n1].astype(jnp.bfloat16), z, xg, bias)


def _hyena_filter_kernel(ft_ref, tn_ref, w1_ref, b1_ref, w2_ref, b2_ref, fr_ref, w3_ref, dl_ref, o_ref,
                         hdn_ref, win_ref):
    cols = o_ref.shape[-1]

    @pl.when(pl.program_id(1) == 0)
    def _():
        pre = jnp.dot(ft_ref[...], w1_ref[...], preferred_element_type=jnp.float32, precision=HIGHEST)
        hdn = jnp.sin(fr_ref[...] * (pre + b1_ref[...]))
        pre = jnp.dot(hdn, w2_ref[...], preferred_element_type=jnp.float32, precision=HIGHEST)
        hdn_ref[...] = jnp.sin(fr_ref[...] * (pre + b2_ref[...])).astype(hdn_ref.dtype)
        tn = tn_ref[...]
        for lt in range(cols // LANES):
            ch = slice(lt * LANES, (lt + 1) * LANES)
            win_ref[:, ch] = jnp.exp(-tn * dl_ref[:, ch]) + HY_SHIFT

    filt = jnp.dot(hdn_ref[...], w3_ref[0].astype(jnp.bfloat16), preferred_element_type=jnp.float32)
    o_ref[0] = filt * win_ref[...]


def hyena_filter_taps(L, w1, b1, w2, b2, w3, freq, tr=512):
    assert FFT_N == 2 * L
    n = np.arange(FFT_N)
    t = np.where(n < L, n, (FFT_N - n) % L).astype(np.float32)
    tnorm = t / np.float32(L)
    bands = np.linspace(1e-4, HY_BANDS - 1, HY_BANDS, dtype=np.float32)
    ang = (np.float32(2.0 * math.pi / L) * t[:, None] * bands[None, :]).astype(np.float64)
    feats = np.concatenate([tnorm[:, None], np.cos(ang), -np.sin(ang)], axis=-1)
    feats = np.pad(feats, ((0, 0), (0, HY_FEAT_PAD - feats.shape[1]))).astype(np.float32)
    tmark = np.broadcast_to(tnorm[:, None], (FFT_N, LANES))
    deltas = np.abs(np.linspace(HY_MIN_DECAY, HY_MAX_DECAY, HY_W, dtype=np.float32))[None, :]
    w1p = jnp.pad(w1, ((0, HY_FEAT_PAD - w1.shape[0]), (0, 0)))
    ffn = w2.shape[0]
    w3r = w3.reshape(ffn, 2 * HY_ORDER, HY_W).transpose(1, 0, 2)
    half = L // tr
    row = lambda r, o: (r, 0)
    fix = lambda r, o: (0, 0)
    return pl.pallas_call(
        _hyena_filter_kernel,
        grid=(FFT_N // tr, HY_ORDER),
        in_specs=[pl.BlockSpec((tr, HY_FEAT_PAD), row),
                  pl.BlockSpec((tr, LANES), row),
                  pl.BlockSpec((HY_FEAT_PAD, ffn), fix),
                  pl.BlockSpec((1, ffn), fix),
                  pl.BlockSpec((ffn, ffn), fix),
                  pl.BlockSpec((1, ffn), fix),
                  pl.BlockSpec((1, ffn), fix),
                  pl.BlockSpec((1, ffn, HY_W), lambda r, o: ((r // half) * HY_ORDER + o, 0, 0)),
                  pl.BlockSpec((1, HY_W), fix)],
        out_specs=pl.BlockSpec((1, tr, HY_W), lambda r, o: (o, r, 0)),
        out_shape=jax.ShapeDtypeStruct((HY_ORDER, FFT_N, HY_W), jnp.float32),
        scratch_shapes=[pltpu.VMEM((tr, ffn), jnp.bfloat16), pltpu.VMEM((tr, HY_W), jnp.float32)],
        compiler_params=pltpu.CompilerParams(
            dimension_semantics=("parallel", "arbitrary"), vmem_limit_bytes=VMEM_LIMIT_BYTES),
    )(jnp.asarray(feats), jnp.asarray(tmark), w1p, b1[None], w2, b2[None], freq[None], w3r, jnp.asarray(deltas))


def hyena(xs, taps, bias):
    L = xs.shape[0]
    C = xs.shape[1] // 3
    first, mid_fwd, mid_inv, last, tcos, tsin = _dft_tables()
    tc = jnp.broadcast_to(tcos[:, :, None], (FFT_R, FFT_R, LANES))
    ts = jnp.broadcast_to(tsin[:, :, None], (FFT_R, FFT_R, LANES))
    shape4 = (2, FFT_R, FFT_R, C)
    bf = jnp.bfloat16
    taps2 = taps.reshape(HY_ORDER * FFT_N, C)
    z, z_col0 = xs, 2 * C
    for o in range(HY_ORDER):
        ga = fft_first(taps2, FFT_N, o, 0, C, first)
        g = fft_mid(ga.reshape(shape4), tc, ts, mid_fwd, mid_inv, out_dtype=bf)
        a = fft_first(z, L, 0, z_col0, C, first)
        q = fft_mid(a.reshape(shape4), tc, ts, mid_fwd, mid_inv, g=g, out_dtype=bf)
        bias_eff = (bias[o] + taps[o, L])[None, :]
        z = fft_last(q.reshape(2 * FFT_R, FFT_R, C), last, z, z_col0, xs, o * C, bias_eff,
                     jnp.float32 if o + 1 < HY_ORDER else bf)
        z_col0 = 0
    return z


_CAND_ROWS = tuple(PEER_TOPK // (i + 1) for i in range(PEER_TOPK))


def _extract_top(s, rounds):
    rows = s.shape[0]
    iota = lax.broadcasted_iota(jnp.int32, s.shape, 0).astype(jnp.float32)
    tops = []
    for _ in range(rounds):
        m = jnp.max(s, axis=0, keepdims=True)
        first = jnp.min(jnp.where(s == m, iota, float(rows)), axis=0, keepdims=True)
        s = jnp.where(iota == first, NEG_INF, s)
        tops.append(m)
    return tops


def _peer_topk_kernel(q_ref, k_ref, s1_ref, s2_ref, e1_ref, e2_ref, thr_ref):
    tk = q_ref.shape[0]
    for u in range(tk // LANES):
        tok = slice(u * LANES, (u + 1) * LANES)
        s = []
        tops = []
        for p in range(2):
            qp = q_ref[tok, p * PEER_NKEYS:(p + 1) * PEER_NKEYS]
            sp = lax.dot_general(k_ref[0, p], qp, (((1,), (1,)), ((), ())),
                                 preferred_element_type=jnp.float32)
            s.append(sp)
            tops.append(_extract_top(sp, PEER_TOPK))
        top2a = jnp.concatenate(tops[1][:8], axis=0)
        top2b = jnp.concatenate(tops[1][8:], axis=0)
        row = lax.broadcasted_iota(jnp.int32, (8, LANES), 0)
        cands = [tops[0][0] + top2a, tops[0][0] + top2b]
        for i in range(1, PEER_TOPK):
            cands.append(jnp.where(row < _CAND_ROWS[i], tops[0][i] + top2a, NEG_INF))
        best = _extract_top(jnp.concatenate(cands, axis=0), PEER_TOPK)
        z = jnp.ones_like(best[0])
        for b in best[1:]:
            z = z + jnp.exp(b - best[0])
        s1_ref[0, :, tok] = s[0]
        s2_ref[0, :, tok] = s[1]
        e1_ref[0, :, tok] = jnp.exp(s[0] - tops[0][0]) / z
        e2_ref[0, :, tok] = jnp.exp(s[1] - tops[1][0])
        thr_ref[0, :, tok] = best[-1]


def peer_topk(q, sub_keys, tk=256):
    T = q.shape[0]
    H = PEER_HEADS
    big = jax.ShapeDtypeStruct((H, PEER_NKEYS, T), jnp.float32)
    big_spec = pl.BlockSpec((1, PEER_NKEYS, tk), lambda i, h: (h, 0, i))
    s1, s2, e1, e2, thr = pl.pallas_call(
        _peer_topk_kernel,
        grid=(T // tk, H),
        in_specs=[pl.BlockSpec((tk, PEER_DKEY), lambda i, h: (i, h)),
                  pl.BlockSpec((1, 2, PEER_NKEYS, PEER_DKEY // 2), lambda i, h: (h, 0, 0, 0))],
        out_specs=[big_spec, big_spec, big_spec, big_spec,
                   pl.BlockSpec((1, 1, tk), lambda i, h: (h, 0, i))],
        out_shape=[big, big, big, big, jax.ShapeDtypeStruct((H, 1, T), jnp.float32)],
        compiler_params=pltpu.CompilerParams(
            dimension_semantics=("parallel", "parallel"),
            vmem_limit_bytes=VMEM_LIMIT_BYTES),
    )(q, sub_keys.astype(jnp.bfloat16))
    return s1, s2, e1, e2, thr.reshape(H, T)


W_ROWS = 16


def _peer_expert_kernel(v_ref, u_ref, vt_ref, s1a_ref, e1a_ref, s2_ref, e2_ref, thr_ref, o_ref, sc_ref, w_ref):
    j = pl.program_id(1)
    tt = v_ref.shape[0]
    na = u_ref.shape[0] // PEER_NKEYS

    def scores():
        sc_ref[...] = lax.dot_general(u_ref[...], v_ref[...], (((1,), (1,)), ((), ())),
                                      preferred_element_type=jnp.float32)

    def weights(al, w_ref):
        for u in range(tt // LANES):
            tok = slice(u * LANES, (u + 1) * LANES)
            for part in range(PEER_NKEYS // W_ROWS):
                b = slice(part * W_ROWS, (part + 1) * W_ROWS)
                g = jnp.zeros((W_ROWS, LANES), jnp.float32)
                for h in range(PEER_HEADS):
                    s1row = s1a_ref[al, h:h + 1, tok]
                    e1row = e1a_ref[al, h:h + 1, tok]
                    keep = (s1row + s2_ref[h, b, tok]) >= thr_ref[h:h + 1, tok]
                    g = g + jnp.where(keep, e1row * e2_ref[h, b, tok], 0.0)
                rows = pl.ds(pl.multiple_of(al * PEER_NKEYS + part * W_ROWS, W_ROWS), W_ROWS)
                sc = sc_ref[rows, tok]
                act = 0.5 * sc * (1.0 + lax.erf(sc * math.sqrt(0.5)))
                w_ref[rows, tok] = (act * g).astype(w_ref.dtype)

    @pl.when(j == 0)
    def _():
        o_ref[...] = jnp.zeros_like(o_ref)

    scores()

    def body(al, carry):
        weights(al, w_ref)
        return carry
    lax.fori_loop(0, na, body, 0)
    o_ref[...] += jnp.dot(vt_ref[...], w_ref[...], preferred_element_type=jnp.float32)


def peer_experts(v, expert_u, expert_vt, s1a, e1a, s2, e2, thr, tt=512, nb=1024):
    T, D = v.shape
    N = expert_u.shape[0]
    na = nb // PEER_NKEYS
    H = PEER_HEADS
    return pl.pallas_call(
        _peer_expert_kernel,
        grid=(T // tt, N // nb),
        in_specs=[pl.BlockSpec((tt, D), lambda i, j: (i, 0)),
                  pl.BlockSpec((nb, D), lambda i, j: (j, 0)),
                  pl.BlockSpec((D, nb), lambda i, j: (0, j)),
                  pl.BlockSpec((na, H, tt), lambda i, j: (j, 0, i)),
                  pl.BlockSpec((na, H, tt), lambda i, j: (j, 0, i)),
                  pl.BlockSpec((H, PEER_NKEYS, tt), lambda i, j: (0, 0, i)),
                  pl.BlockSpec((H, PEER_NKEYS, tt), lambda i, j: (0, 0, i)),
                  pl.BlockSpec((H, tt), lambda i, j: (0, i))],
        out_specs=pl.BlockSpec((D, tt), lambda i, j: (0, i)),
        out_shape=jax.ShapeDtypeStruct((D, T), jnp.float32),
        scratch_shapes=[pltpu.VMEM((nb, tt), jnp.float32), pltpu.VMEM((nb, tt), jnp.bfloat16)],
        compiler_params=pltpu.CompilerParams(
            dimension_semantics=("parallel", "arbitrary"),
            vmem_limit_bytes=VMEM_LIMIT_BYTES),
    )(v, expert_u, expert_vt, s1a, e1a, s2, e2, thr)


def peer(u, w_q, sub_keys, expert_u, expert_v, tk=256, tt=512, nb=1024):
    ub = u.astype(jnp.bfloat16)
    q = pmm(ub, w_q.astype(jnp.bfloat16), out_dtype=jnp.bfloat16)
    s1, s2, e1, e2, thr = peer_topk(q, sub_keys, tk=tk)
    out_t = peer_experts(ub, expert_u.astype(jnp.bfloat16), expert_v.astype(jnp.bfloat16).T,
                         s1.transpose(1, 0, 2), e1.transpose(1, 0, 2), s2, e2, thr, tt=tt, nb=nb)
    return out_t.T


ROW_BLK = 256
STATE_MIN = -1e30

COL_QK, COL_V, COL_O = 0, 2 * ML_QK_W, 2 * ML_QK_W + ML_V_W
COL_HY = COL_O + ML_V_W
COL_BG = COL_HY + 3 * HY_W
COL_GATE = COL_BG + 2 * D_MODEL
IN_W_PAD = 16896


def _params(*sem):
    return pltpu.CompilerParams(dimension_semantics=sem, vmem_limit_bytes=VMEM_LIMIT_BYTES)


def _norm_mod_kernel(x_ref, c_ref, g_ref, sc_ref, sh_ref, o_ref, *, n_lat_blk):
    def emit(x):
        y = x * lax.rsqrt(jnp.mean(x * x, axis=-1, keepdims=True) + EPS)
        o_ref[...] = (y * g_ref[...] * (1.0 + sc_ref[0]) + sh_ref[0]).astype(o_ref.dtype)

    @pl.when(pl.program_id(0) < n_lat_blk)
    def _():
        emit(x_ref[...])

    @pl.when(pl.program_id(0) >= n_lat_blk)
    def _():
        emit(c_ref[...])


def norm_mod(x, ctx, g, sc, sh):
    L, D = x.shape
    n_lat_blk = L // ROW_BLK
    mod = pl.BlockSpec((1, 1, D), lambda i: (i // n_lat_blk, 0, 0))
    return pl.pallas_call(
        functools.partial(_norm_mod_kernel, n_lat_blk=n_lat_blk),
        grid=(n_lat_blk + 1,),
        in_specs=[pl.BlockSpec((ROW_BLK, D), lambda i: (jnp.minimum(i, n_lat_blk - 1), 0)),
                  pl.BlockSpec((ROW_BLK, D), lambda i: (0, 0)),
                  pl.BlockSpec((1, D), lambda i: (0, 0)), mod, mod],
        out_specs=pl.BlockSpec((ROW_BLK, D), lambda i: (i, 0)),
        out_shape=jax.ShapeDtypeStruct((L + ROW_BLK, D), jnp.bfloat16),
        compiler_params=_params("parallel"),
    )(x, ctx, g[None], sc, sh)


def _fill_halo(buf_ref, prev_ref, cur_ref, next_ref):
    r = cur_ref.shape[0]
    h = prev_ref.shape[0]
    buf_ref[0:h, :] = prev_ref[...]
    buf_ref[h:h + r, :] = cur_ref[...]
    buf_ref[h + r:2 * h + r, :] = next_ref[...]
    return h


def _qk_conv_kernel(prev_ref, cur_ref, next_ref, w_ref, o_ref, buf_ref, *, n_lat_blk, q_slabs):
    j = pl.program_id(0)
    i = pl.program_id(1)
    r, cw = cur_ref.shape
    halo = _fill_halo(buf_ref, prev_ref, cur_ref, next_ref)
    local = lax.broadcasted_iota(jnp.int32, (r, cw), 0)
    n_lat = n_lat_blk * r

    def tap(off):
        return buf_ref[pl.ds(halo + off, r), :]

    def finish(acc):
        y = acc * jax.nn.sigmoid(acc)
        scale = jnp.where(j < q_slabs, ML_DQK ** -0.5, 1.0)
        o_ref[...] = (y * scale).astype(o_ref.dtype)

    @pl.when(i < n_lat_blk)
    def _():
        g = i * r + local
        col = jnp.bitwise_and(g, GRID_W - 1)
        acc = jnp.zeros((r, cw), jnp.float32)
        for dr in (-1, 0, 1):
            for dc in (-1, 0, 1):
                t = tap(GRID_W * dr + dc)
                if dr == -1:
                    t = jnp.where(g >= GRID_W, t, 0.0)
                if dr == 1:
                    t = jnp.where(g < n_lat - GRID_W, t, 0.0)
                if dc == -1:
                    t = jnp.where(col >= 1, t, 0.0)
                if dc == 1:
                    t = jnp.where(col < GRID_W - 1, t, 0.0)
                acc = acc + t * w_ref[3 * (dr + 1) + dc + 1:3 * (dr + 1) + dc + 2, :]
        finish(acc)

    @pl.when(i >= n_lat_blk)
    def _():
        acc = tap(0) * w_ref[4:5, :]
        acc = acc + jnp.where(local >= 1, tap(-1), 0.0) * w_ref[3:4, :]
        acc = acc + jnp.where(local < r - 1, tap(1), 0.0) * w_ref[5:6, :]
        finish(acc)


def _halo_specs(cw, col0_blk, nblk, halo):
    per = ROW_BLK // halo
    cur = lambda j, i: (i, col0_blk + j)
    prev = lambda j, i: (jnp.maximum(i * per - 1, 0), col0_blk + j)
    nxt = lambda j, i: (jnp.minimum((i + 1) * per, nblk * per - 1), col0_blk + j)
    return [pl.BlockSpec((halo, cw), prev), pl.BlockSpec((ROW_BLK, cw), cur), pl.BlockSpec((halo, cw), nxt)]


QK_HALO = 128
SEQ_HALO = 8


def qk_conv(p, w9, n_lat_blk, cw=512):
    T = p.shape[0]
    nblk = T // ROW_BLK
    width = 2 * ML_QK_W
    return pl.pallas_call(
        functools.partial(_qk_conv_kernel, n_lat_blk=n_lat_blk, q_slabs=ML_QK_W // cw),
        grid=(width // cw, nblk),
        in_specs=_halo_specs(cw, COL_QK // cw, nblk, QK_HALO) + [pl.BlockSpec((9, cw), lambda j, i: (0, j))],
        out_specs=pl.BlockSpec((ROW_BLK, cw), lambda j, i: (i, j)),
        out_shape=jax.ShapeDtypeStruct((T, width), jnp.bfloat16),
        scratch_shapes=[pltpu.VMEM((ROW_BLK + 2 * QK_HALO, cw), jnp.float32)],
        compiler_params=_params("parallel", "parallel"),
    )(p, p, p, w9)


def _seq_conv_kernel(prev_ref, cur_ref, next_ref, w_ref, o_ref, buf_ref, *, n_rows):
    i = pl.program_id(1)
    r, cw = cur_ref.shape
    halo = _fill_halo(buf_ref, prev_ref, cur_ref, next_ref)
    g = i * r + lax.broadcasted_iota(jnp.int32, (r, cw), 0)
    acc = buf_ref[pl.ds(halo, r), :] * w_ref[1:2, :]
    acc = acc + jnp.where(g >= 1, buf_ref[pl.ds(halo - 1, r), :], 0.0) * w_ref[0:1, :]
    acc = acc + jnp.where(g < n_rows - 1, buf_ref[pl.ds(halo + 1, r), :], 0.0) * w_ref[2:3, :]
    o_ref[...] = acc.astype(o_ref.dtype)


def hy_conv(p, w3, n_lat_blk, cw=1024):
    nblk = p.shape[0] // ROW_BLK
    width = 3 * HY_W
    n_rows = n_lat_blk * ROW_BLK
    return pl.pallas_call(
        functools.partial(_seq_conv_kernel, n_rows=n_rows),
        grid=(width // cw, n_lat_blk),
        in_specs=_halo_specs(cw, COL_HY // cw, nblk, SEQ_HALO) + [pl.BlockSpec((3, cw), lambda j, i: (0, j))],
        out_specs=pl.BlockSpec((ROW_BLK, cw), lambda j, i: (i, j)),
        out_shape=jax.ShapeDtypeStruct((n_rows, width), jnp.float32),
        scratch_shapes=[pltpu.VMEM((ROW_BLK + 2 * SEQ_HALO, cw), jnp.float32)],
        compiler_params=_params("parallel", "parallel"),
    )(p, p, p, w3)


def _mlstm_kernel(q_ref, k_ref, kt_ref, v_ref, g_ref, o_ref, c_ref, n_ref, m_ref):
    d = pl.program_id(0)
    step = pl.program_id(2)
    r = q_ref.shape[0]

    @pl.when(step == 0)
    def _():
        c_ref[...] = jnp.zeros_like(c_ref)
        n_ref[...] = jnp.zeros_like(n_ref)
        m_ref[...] = jnp.full_like(m_ref, STATE_MIN)

    ig = g_ref[0, 0, 0:1, :]
    fg = g_ref[0, 0, 1:2, :]
    t_i = lax.broadcasted_iota(jnp.int32, (r, r), 0)
    s_i = lax.broadcasted_iota(jnp.int32, (r, r), 1)
    sgn = jnp.where(d == 0, 1, -1)
    reads = (s_i - t_i) * sgn <= 0
    read_by = (t_i - s_i) * sgn <= 0
    eye = s_i == t_i
    f_col = jnp.sum(jnp.where(eye, fg, 0.0), axis=1, keepdims=True)
    cum_col = jnp.sum(jnp.where(reads, fg, 0.0), axis=1, keepdims=True)
    cum_row = jnp.sum(jnp.where(read_by, f_col, 0.0), axis=0, keepdims=True)
    total = jnp.sum(fg, axis=1, keepdims=True)
    m_old = m_ref[...]

    dmat = jnp.where(reads, (cum_col - cum_row) + ig, NEG_INF)
    inter = cum_col + m_old
    m_t = jnp.maximum(inter, jnp.max(dmat, axis=1, keepdims=True))
    w = jnp.exp(dmat - m_t)
    s_inter = jnp.exp(inter - m_t)
    q = q_ref[...]
    v = v_ref[...]
    qkw = jnp.dot(q, kt_ref[...], preferred_element_type=jnp.float32) * w
    qn = jnp.sum(q.astype(jnp.float32) * n_ref[...], axis=1, keepdims=True)
    den = jnp.sum(qkw, axis=1, keepdims=True) + s_inter * qn
    num = jnp.dot(qkw.astype(jnp.bfloat16), v.astype(jnp.bfloat16), preferred_element_type=jnp.float32) \
        + s_inter * jnp.dot(q, c_ref[...].astype(jnp.bfloat16), preferred_element_type=jnp.float32)
    o_ref[0] = num / jnp.maximum(jnp.abs(den), jnp.exp(-m_t))

    dec = (total - cum_row) + ig
    m_new = jnp.maximum(total + m_old, jnp.max(dec, axis=1, keepdims=True))
    wk_row = jnp.exp(dec - m_new)
    s_old = jnp.exp(total + m_old - m_new)
    wk_col = jnp.sum(jnp.where(eye, wk_row, 0.0), axis=1, keepdims=True)
    c_ref[...] = s_old * c_ref[...] + jnp.dot(kt_ref[...], (wk_col * v).astype(jnp.bfloat16),
                                              preferred_element_type=jnp.float32)
    n_ref[...] = s_old * n_ref[...] + jnp.sum(wk_col * k_ref[...].astype(jnp.float32), axis=0, keepdims=True)
    m_ref[...] = m_new


def mlstm_scan(qk, kt, p, gates, n_lat_blk):
    T = qk.shape[0]
    r = ROW_BLK

    def blk(d, s):
        lat = jnp.where(d == 0, s - 1, n_lat_blk - s)
        return jnp.where(s == 0, n_lat_blk, lat)

    return pl.pallas_call(
        _mlstm_kernel,
        grid=(2, ML_HEADS, n_lat_blk + 1),
        in_specs=[pl.BlockSpec((r, ML_DQK), lambda d, h, s: (blk(d, s), h)),
                  pl.BlockSpec((r, ML_DQK), lambda d, h, s: (blk(d, s), ML_HEADS + h)),
                  pl.BlockSpec((ML_DQK, r), lambda d, h, s: (h, blk(d, s))),
                  pl.BlockSpec((r, ML_DV), lambda d, h, s: (blk(d, s), COL_V // ML_DV + h)),
                  pl.BlockSpec((1, 1, 2, r), lambda d, h, s: (d, h, 0, blk(d, s)))],
        out_specs=pl.BlockSpec((1, r, ML_DV), lambda d, h, s: (d, blk(d, s), h)),
        out_shape=jax.ShapeDtypeStruct((2, T, ML_V_W), jnp.float32),
        scratch_shapes=[pltpu.VMEM((ML_DQK, ML_DV), jnp.float32), pltpu.VMEM((1, ML_DQK), jnp.float32),
                        pltpu.VMEM((1, 1), jnp.float32)],
        compiler_params=_params("parallel", "parallel", "arbitrary"),
    )(qk, qk, kt, p, gates)


def _ml_out_kernel(hf_ref, hb_ref, o_ref, g_ref, out_ref):
    for h in range(ML_HEADS):
        ch = slice(h * ML_DV, (h + 1) * ML_DV)
        hh = hf_ref[0, :, ch] + hb_ref[0, :, ch]
        hh = hh * lax.rsqrt(jnp.mean(hh * hh, axis=-1, keepdims=True) + EPS)
        out_ref[:, ch] = (hh * g_ref[:, ch] * jax.nn.sigmoid(o_ref[:, ch])).astype(out_ref.dtype)


def ml_out(h2, p, gain, n_lat_blk):
    r = ROW_BLK
    return pl.pallas_call(
        _ml_out_kernel,
        grid=(n_lat_blk,),
        in_specs=[pl.BlockSpec((1, r, ML_V_W), lambda i: (0, i, 0)),
                  pl.BlockSpec((1, r, ML_V_W), lambda i: (1, i, 0)),
                  pl.BlockSpec((r, ML_V_W), lambda i: (i, COL_O // ML_V_W)),
                  pl.BlockSpec((1, ML_V_W), lambda i: (0, 0))],
        out_specs=pl.BlockSpec((r, ML_V_W), lambda i: (i, 0)),
        out_shape=jax.ShapeDtypeStruct((n_lat_blk * r, ML_V_W), jnp.bfloat16),
        compiler_params=_params("parallel"),
    )(h2, h2, p, gain[None])


def _merge_kernel(hm_ref, hh_ref, wm_ref, wh_ref, gm_ref, gh_ref, o_ref):
    ym = jnp.dot(hm_ref[...], wm_ref[...], preferred_element_type=jnp.float32)
    yh = jnp.dot(hh_ref[...], wh_ref[...], preferred_element_type=jnp.float32)
    o_ref[...] = (jax.nn.sigmoid(gm_ref[...]) * ym + jax.nn.sigmoid(gh_ref[...]) * yh).astype(o_ref.dtype)


def merge_branches(h_ml, h_hy, w_pm, w_ph, p, tm=512, tn=1024):
    L, K = h_ml.shape
    D = w_pm.shape[1]
    act = pl.BlockSpec((tm, K), lambda j, i: (i, 0))
    wgt = pl.BlockSpec((K, tn), lambda j, i: (0, j))
    return pl.pallas_call(
        _merge_kernel,
        grid=(D // tn, L // tm),
        in_specs=[act, act, wgt, wgt,
                  pl.BlockSpec((tm, tn), lambda j, i: (i, COL_BG // tn + j)),
                  pl.BlockSpec((tm, tn), lambda j, i: (i, (COL_BG + D) // tn + j))],
        out_specs=pl.BlockSpec((tm, tn), lambda j, i: (i, j)),
        out_shape=jax.ShapeDtypeStruct((L, D), jnp.bfloat16),
        compiler_params=_params("parallel", "parallel"),
    )(h_ml, h_hy, w_pm, w_ph, p, p)


def _out_proj_kernel(y_ref, w_ref, x_ref, g1_ref, ng_ref, sc_ref, sh_ref, h_ref, v_ref):
    h = x_ref[...] + g1_ref[...] * jnp.dot(y_ref[...], w_ref[...], preferred_element_type=jnp.float32)
    h_ref[...] = h
    n = h * lax.rsqrt(jnp.mean(h * h, axis=-1, keepdims=True) + EPS)
    v_ref[...] = (n * ng_ref[...] * (1.0 + sc_ref[...]) + sh_ref[...]).astype(v_ref.dtype)


def out_proj(y, w_o, x, g1, ng, sc, sh, tm=256):
    L, D = x.shape
    row = pl.BlockSpec((tm, D), lambda i: (i, 0))
    vec = pl.BlockSpec((1, D), lambda i: (0, 0))
    return pl.pallas_call(
        _out_proj_kernel,
        grid=(L // tm,),
        in_specs=[row, pl.BlockSpec((D, D), lambda i: (0, 0)), row, vec, vec, vec, vec],
        out_specs=[row, row],
        out_shape=[jax.ShapeDtypeStruct((L, D), jnp.float32), jax.ShapeDtypeStruct((L, D), jnp.bfloat16)],
        compiler_params=_params("parallel"),
    )(y, w_o, x, g1, ng, sc, sh)


def _final_kernel(h_ref, p_ref, g2_ref, fg_ref, o_ref):
    h = h_ref[...] + g2_ref[...] * p_ref[...]
    o_ref[...] = h * lax.rsqrt(jnp.mean(h * h, axis=-1, keepdims=True) + EPS) * fg_ref[...]


def final_norm(h, peer_out, g2, fg, tm=256):
    L, D = h.shape
    row = pl.BlockSpec((tm, D), lambda i: (i, 0))
    vec = pl.BlockSpec((1, D), lambda i: (0, 0))
    return pl.pallas_call(
        _final_kernel,
        grid=(L // tm,),
        in_specs=[row, row, vec, vec],
        out_specs=row,
        out_shape=jax.ShapeDtypeStruct((L, D), jnp.float32),
        compiler_params=_params("parallel"),
    )(h, peer_out, g2, fg)


def token_mixer_inputs(x, ctx, mods, norm_g, w_in):
    L = x.shape[0]
    assert ctx.shape[0] == ROW_BLK and L % ROW_BLK == 0
    n_lat_blk = L // ROW_BLK
    sh1, sc1 = mods[:, 0], mods[:, 1]
    u = norm_mod(x, ctx, norm_g, sc1[:, None, :], sh1[:, None, :])
    n_in = w_in.shape[1]
    w_r = jnp.concatenate([w_in[:, :COL_O + ML_V_W], w_in[:, COL_O + ML_V_W + 4 * ML_HEADS:],
                           w_in[:, COL_O + ML_V_W:COL_O + ML_V_W + 4 * ML_HEADS],
                           jnp.zeros((w_in.shape[0], IN_W_PAD - n_in), w_in.dtype)], axis=1).astype(jnp.bfloat16)
    return pmm(u, w_r, tm=768, tn=1536), n_lat_blk


def kernel(x, c, ctx, c_ctx, w_mod, b_mod, norm1_g, norm2_g, final_g, w_in, ml_conv_w, ml_gate_b, ml_norm_g, hy_conv_w, hy_w1, hy_b1, hy_w2, hy_b2, hy_w3, hy_freq, hy_bias, w_proj_ml, w_proj_hy, w_out, peer_wq, peer_keys, peer_u, peer_v):
    l = 0
    L, D = x.shape[1:]
    bf = jnp.bfloat16
    s_rows = jnp.zeros((8, D), jnp.float32).at[0].set(jax.nn.silu(c[0])).at[1].set(jax.nn.silu(c_ctx))
    mods = (pmm(s_rows, w_mod[l], tm=8, tn=1024)[:2] + b_mod[l]).reshape(2, 6, D)

    p, n_lat_blk = token_mixer_inputs(x[0], ctx[0], mods, norm1_g[l], w_in[l])
    T = p.shape[0]

    qk = qk_conv(p, ml_conv_w[l].reshape(9, 2 * ML_QK_W), n_lat_blk)
    kt = qk[:, ML_QK_W:].T
    gt = p[:, COL_GATE:COL_GATE + 4 * ML_HEADS].reshape(T, 4, ML_HEADS) + ml_gate_b[l]
    gates = jnp.stack([jnp.stack([gt[:, 0], jax.nn.log_sigmoid(gt[:, 1])]),
                       jnp.stack([gt[:, 2], jax.nn.log_sigmoid(gt[:, 3])])]).transpose(0, 3, 1, 2)
    h2 = mlstm_scan(qk, kt, p, gates, n_lat_blk)
    h_ml = ml_out(h2, p, ml_norm_g[l], n_lat_blk)

    xs = hy_conv(p, hy_conv_w[l], n_lat_blk)
    taps = hyena_filter_taps(L, hy_w1[l], hy_b1[l], hy_w2[l], hy_b2[l], hy_w3[l], hy_freq[l])
    h_hy = hyena(xs, taps, hy_bias[l])

    y = merge_branches(h_ml, h_hy.astype(bf), w_proj_ml[l].astype(bf), w_proj_hy[l].astype(bf), p)
    h_lat, v_lat = out_proj(y, w_out[l].astype(bf), x[0], mods[0:1, 2], norm2_g[l][None], mods[0:1, 4], mods[0:1, 3])
    peer_out = peer(v_lat, peer_wq[l], peer_keys[l], peer_u[l], peer_v[l])
    return final_norm(h_lat, peer_out, mods[0:1, 5], final_g[None])[None]
```
